```python
import jax, jax.numpy as jnp
from jax import lax
import numpy as np

D_MODEL = 2048
BATCH = 1
SEQ = 8192
DEPTH = 2
DEC_BATCH = 32
DEC_SEQ = 4
PAST_LEN = 8192
PAGE_SIZE = 128

N_HEADS = 16
HEAD_DIM = D_MODEL // N_HEADS
ATTN_WIDTH = N_HEADS * HEAD_DIM
ATTN_SCALE = HEAD_DIM ** -0.5
DILATED_GROUPS = ((128, 1), (512, 4), (2048, 16))
N_GROUPS = len(DILATED_GROUPS)
N_MIXERS = 2
N_A_LAYERS = (DEPTH + N_MIXERS - 1) // N_MIXERS
N_B_LAYERS = DEPTH // N_MIXERS
D_FF = ((8 * D_MODEL // 3 + 255) // 256) * 256
BLOCK = 128
ROPE_THETA = 10000.0
RMS_EPS = 1e-6
SB_BIAS_INIT = -8.0

kernel_name = 'dilated_stickbreaking_macaron_decode_step'


def rms_norm(x, g):
    xf = x.astype(jnp.float32)
    y = xf * lax.rsqrt(jnp.mean(xf * xf, axis=-1, keepdims=True) + RMS_EPS)
    return (y * g.astype(jnp.float32)).astype(x.dtype)


def swiglu(h, w_gu, w_d):
    g, u = jnp.split(h @ w_gu, 2, axis=-1)
    return (jax.nn.silu(g) * u) @ w_d


def macaron_half(x, g, w_gu, w_d):
    return x + 0.5 * swiglu(rms_norm(x, g), w_gu, w_d)


def rope(x, pos):
    half = x.shape[-1] // 2
    inv = ROPE_THETA ** (-jnp.arange(half, dtype=jnp.float32) / half)
    ang = pos.astype(jnp.float32)[:, None] * inv[None, :]
    bshape = (ang.shape[0],) + (1,) * (x.ndim - 3) + (half,)
    cos = jnp.cos(ang).reshape(bshape)
    sin = jnp.sin(ang).reshape(bshape)
    xf = x.astype(jnp.float32)
    x1, x2 = xf[..., :half], xf[..., half:]
    return jnp.concatenate([x1 * cos - x2 * sin, x2 * cos + x1 * sin], axis=-1).astype(x.dtype)


def dilated_project(h, w_in, pos):
    b, s, _ = h.shape
    p = (h @ w_in).reshape(b, s, N_GROUPS, 3, N_HEADS, HEAD_DIM)
    return rope(p[:, :, :, 0], pos), rope(p[:, :, :, 1], pos), p[:, :, :, 2]


def dilated_band_attention(q, k, v, dil, n_back):
    b, s = q.shape[:2]
    span = dil * BLOCK
    s_pad = -(-s // span) * span
    n_cls = s_pad // dil
    n_blk = n_cls // BLOCK

    def to_blocks(t):
        t = jnp.pad(t, ((0, 0), (0, s_pad - s), (0, 0), (0, 0)))
        t = t.reshape(b, n_cls, dil, N_HEADS, HEAD_DIM).transpose(0, 2, 1, 3, 4)
        return t.reshape(b, dil, n_blk, BLOCK, N_HEADS, HEAD_DIM)

    def with_previous(t):
        prev = jnp.pad(t, ((0, 0), (0, 0), (1, 0), (0, 0), (0, 0), (0, 0)))[:, :, :-1]
        return jnp.concatenate([prev, t], axis=3)

    qb = to_blocks(q)
    kb = with_previous(to_blocks(k))
    vb = with_previous(to_blocks(v))
    sc = jnp.einsum('bcnqhd,bcnkhd->bcnqhk', qb, kb).astype(jnp.float32) * ATTN_SCALE
    qi = jnp.arange(BLOCK)[:, None]
    kj = jnp.arange(2 * BLOCK)[None, :]
    dist = BLOCK + qi - kj
    band = (dist >= 0) & (dist <= n_back)
    has_prev = (jnp.arange(n_blk) > 0)[:, None, None] | (kj >= BLOCK)[None]
    mask = (band[None] & has_prev)[:, :, None, :]
    sc = jnp.where(mask, sc, -jnp.inf)
    lse = jax.nn.logsumexp(sc, axis=-1)
    p = jnp.exp(sc - lse[..., None]).astype(v.dtype)
    o = jnp.einsum('bcnqhk,bcnkhd->bcnqhd', p, vb)
    o = o.reshape(b, dil, n_cls, N_HEADS, HEAD_DIM).transpose(0, 2, 1, 3, 4)
    o = o.reshape(b, s_pad, N_HEADS, HEAD_DIM)[:, :s]
    lse = lse.reshape(b, dil, n_cls, N_HEADS).transpose(0, 2, 1, 3).reshape(b, s_pad, N_HEADS)[:, :s]
    return o, lse


def dilated_window_step(q, kv_new, buf, dil, window):
    n_buf, t = buf.shape[1], q.shape[1]
    kv = jnp.concatenate([buf, kv_new], axis=1)
    idx = n_buf + jnp.arange(t)[:, None] - dil * jnp.arange(window // dil + 1)[None, :]
    valid = idx >= 0
    g = kv[:, jnp.maximum(idx, 0)]
    sc = jnp.einsum('bthd,btjhd->bthj', q, g[:, :, :, 0]).astype(jnp.float32) * ATTN_SCALE
    sc = jnp.where(valid[:, None, :], sc, -jnp.inf)
    lse = jax.nn.logsumexp(sc, axis=-1)
    p = jnp.exp(sc - lse[..., None]).astype(q.dtype)
    o = jnp.einsum('bthj,btjhd->bthd', p, g[:, :, :, 1])
    return o, lse, kv[:, -min(window, n_buf + t):]


def merge_groups(outs, lses, w_out):
    wts = jax.nn.softmax(jnp.stack(lses), axis=0)
    o = jnp.einsum('gbsh,gbshd->bshd', wts, jnp.stack(outs).astype(jnp.float32))
    b, s = o.shape[:2]
    return o.astype(outs[0].dtype).reshape(b, s, ATTN_WIDTH) @ w_out


def dilated_prompt(h, w_in, w_out):
    s = h.shape[1]
    q, k, v = dilated_project(h, w_in, jnp.arange(s))
    outs, lses, states = [], [], []
    for g, (window, dil) in enumerate(DILATED_GROUPS):
        o, lse = dilated_band_attention(q[:, :, g], k[:, :, g], v[:, :, g], dil, window // dil)
        outs.append(o)
        lses.append(lse)
        n_keep = min(window, s)
        states.append(jnp.stack([k[:, -n_keep:, g], v[:, -n_keep:, g]], axis=2))
    return merge_groups(outs, lses, w_out), states


def dilated_sample(h, bufs, w_in, w_out):
    t = h.shape[1]
    q, k, v = dilated_project(h, w_in, PAST_LEN + jnp.arange(t))
    outs, lses, states = [], [], []
    for g, (window, dil) in enumerate(DILATED_GROUPS):
        kv_new = jnp.stack([k[:, :, g], v[:, :, g]], axis=2)
        o, lse, st = dilated_window_step(q[:, :, g], kv_new, bufs[g], dil, window)
        outs.append(o)
        lses.append(lse)
        states.append(st)
    return merge_groups(outs, lses, w_out), states


def stick_breaking_weights(z, causal):
    u = jnp.where(causal, jax.nn.log_sigmoid(-z), 0.0)
    after = lax.cumsum(u, axis=z.ndim - 1, reverse=True) - u
    return jnp.where(causal, jnp.exp(jax.nn.log_sigmoid(z) + after), 0.0)


def stick_breaking_prompt(h, w_qkv, b_logit, w_out):
    b, s, _ = h.shape
    p = (h @ w_qkv).reshape(b, s, 3, N_HEADS, HEAD_DIM)
    q, k, v = p[:, :, 0], p[:, :, 1], p[:, :, 2]
    n_blk = s // BLOCK
    q_blocks = q.reshape(b, n_blk, BLOCK, N_HEADS, HEAD_DIM).swapaxes(0, 1)
    k_pos = jnp.arange(s)
    bias = b_logit.astype(jnp.float32)[None, :, None, None]

    def one_block(args):
        qb, bi = args
        q_pos = bi * BLOCK + jnp.arange(BLOCK)
        z = jnp.einsum('bqhd,bkhd->bhqk', qb, k).astype(jnp.float32) * ATTN_SCALE + bias
        a = stick_breaking_weights(z, k_pos[None, :] < q_pos[:, None])
        return jnp.einsum('bhqk,bkhd->bqhd', a.astype(v.dtype), v)

    o = lax.map(one_block, (q_blocks, jnp.arange(n_blk)))
    o = o.swapaxes(0, 1).reshape(b, s, ATTN_WIDTH)
    return o @ w_out, k, v


def stick_breaking_sample(h, cache_k, cache_v, layer, page_table, w_qkv, b_logit, w_out):
    b, t, _ = h.shape
    p = (h @ w_qkv).reshape(b, t, 3, N_HEADS, HEAD_DIM)
    q, k, v = p[:, :, 0], p[:, :, 1], p[:, :, 2]
    k_past = cache_k[layer][page_table].reshape(b, -1, N_HEADS, HEAD_DIM)
    v_past = cache_v[layer][page_table].reshape(b, -1, N_HEADS, HEAD_DIM)
    n_past = k_past.shape[1]
    bias = b_logit.astype(jnp.float32)[None, :, None, None]
    z = jnp.concatenate([jnp.einsum('bthd,bphd->bhtp', q, k_past),
                         jnp.einsum('bthd,bshd->bhts', q, k)], axis=-1).astype(jnp.float32) * ATTN_SCALE + bias
    causal = jnp.arange(n_past + t)[None, :] < (n_past + jnp.arange(t))[:, None]
    a = stick_breaking_weights(z, causal).astype(v.dtype)
    o = (jnp.einsum('bhtp,bphd->bthd', a[..., :n_past], v_past)
         + jnp.einsum('bhts,bshd->bthd', a[..., n_past:], v))
    return o.reshape(b, t, ATTN_WIDTH) @ w_out, k, v


def setup_inputs(seed: int = 0) -> dict:
    key = jax.random.key(seed)
    ks = jax.random.split(key, 21)
    f32 = jnp.float32

    def normal(i, shape, scale=1.0):
        return scale * jax.random.normal(ks[i], shape, f32)

    def gain(i, shape):
        return 1.0 + 0.02 * jax.random.normal(ks[i], shape, f32)

    n_pages = PAST_LEN // PAGE_SIZE
    n_used = DEC_BATCH * n_pages
    n_phys = n_used + max(1, n_used // 4)
    page_table = jax.random.permutation(ks[7], n_phys)[:n_used].reshape(DEC_BATCH, n_pages).astype(jnp.int32)
    (w0, _), (w1, _), (w2, _) = DILATED_GROUPS
    return {
        'x_prompt': normal(0, (BATCH, SEQ, D_MODEL)),
        'x_sample': normal(1, (DEC_BATCH, DEC_SEQ, D_MODEL)),
        'state_win128_kv': normal(2, (N_A_LAYERS, DEC_BATCH, min(w0, PAST_LEN), 2, N_HEADS, HEAD_DIM)),
        'state_win512_kv': normal(3, (N_A_LAYERS, DEC_BATCH, min(w1, PAST_LEN), 2, N_HEADS, HEAD_DIM)),
        'state_win2048_kv': normal(4, (N_A_LAYERS, DEC_BATCH, min(w2, PAST_LEN), 2, N_HEADS, HEAD_DIM)),
        'cache_sb_k': normal(5, (N_B_LAYERS, n_phys, PAGE_SIZE, N_HEADS, HEAD_DIM)),
        'cache_sb_v': normal(6, (N_B_LAYERS, n_phys, PAGE_SIZE, N_HEADS, HEAD_DIM)),
        'page_table': page_table,
        'norm_pre': gain(8, (DEPTH, D_MODEL)),
        'ffn_pre_wgu': normal(9, (DEPTH, D_MODEL, 2 * D_FF), D_MODEL ** -0.5),
        'ffn_pre_wd': normal(10, (DEPTH, D_FF, D_MODEL), D_FF ** -0.5),
        'norm_mix': gain(11, (DEPTH, D_MODEL)),
        'a_w_in': normal(12, (N_A_LAYERS, D_MODEL, N_GROUPS * 3 * ATTN_WIDTH), D_MODEL ** -0.5),
        'a_w_out': normal(13, (N_A_LAYERS, ATTN_WIDTH, D_MODEL), ATTN_WIDTH ** -0.5),
        'b_w_qkv': normal(14, (N_B_LAYERS, D_MODEL, 3 * ATTN_WIDTH), D_MODEL ** -0.5),
        'b_logit_bias': SB_BIAS_INIT + normal(20, (N_B_LAYERS, N_HEADS), 0.5),
        'b_w_out': normal(15, (N_B_LAYERS, ATTN_WIDTH, D_MODEL), ATTN_WIDTH ** -0.5),
        'norm_post': gain(16, (DEPTH, D_MODEL)),
        'ffn_post_wgu': normal(17, (DEPTH, D_MODEL, 2 * D_FF), D_MODEL ** -0.5),
        'ffn_post_wd': normal(18, (DEPTH, D_FF, D_MODEL), D_FF ** -0.5),
        'final_norm': gain(19, (D_MODEL,)),
    }


def reference(x_prompt, x_sample, state_win128_kv, state_win512_kv, state_win2048_kv,
              cache_sb_k, cache_sb_v, page_table, norm_pre, ffn_pre_wgu, ffn_pre_wd,
              norm_mix, a_w_in, a_w_out, b_w_qkv, b_logit_bias, b_w_out, norm_post, ffn_post_wgu,
              ffn_post_wd, final_norm):
    yp, ys = x_prompt, x_sample
    win_bufs = (state_win128_kv, state_win512_kv, state_win2048_kv)
    win_p = [[] for _ in DILATED_GROUPS]
    win_s = [[] for _ in DILATED_GROUPS]
    sbk_p, sbv_p, sbk_s, sbv_s = [], [], [], []
    for i in range(DEPTH):
        j = i // N_MIXERS
        yp = macaron_half(yp, norm_pre[i], ffn_pre_wgu[i], ffn_pre_wd[i])
        ys = macaron_half(ys, norm_pre[i], ffn_pre_wgu[i], ffn_pre_wd[i])
        hp = rms_norm(yp, norm_mix[i])
        hs = rms_norm(ys, norm_mix[i])
        if i % N_MIXERS == 0:
            mp, st_p = dilated_prompt(hp, a_w_in[j], a_w_out[j])
            ms, st_s = dilated_sample(hs, tuple(buf[j] for buf in win_bufs), a_w_in[j], a_w_out[j])
            for g in range(N_GROUPS):
                win_p[g].append(st_p[g])
                win_s[g].append(st_s[g])
        else:
            mp, kp, vp = stick_breaking_prompt(hp, b_w_qkv[j], b_logit_bias[j], b_w_out[j])
            ms, kn, vn = stick_breaking_sample(hs, cache_sb_k, cache_sb_v, j, page_table,
                                               b_w_qkv[j], b_logit_bias[j], b_w_out[j])
            sbk_p.append(kp)
            sbv_p.append(vp)
            sbk_s.append(kn)
            sbv_s.append(vn)
        yp = yp + mp
        ys = ys + ms
        yp = macaron_half(yp, norm_post[i], ffn_post_wgu[i], ffn_post_wd[i])
        ys = macaron_half(ys, norm_post[i], ffn_post_wgu[i], ffn_post_wd[i])
    y_prompt = rms_norm(yp, final_norm)
    y_sample = rms_norm(ys, final_norm)
    return (y_prompt, y_sample,
            jnp.stack(win_p[0]), jnp.stack(win_s[0]),
            jnp.stack(win_p[1]), jnp.stack(win_s[1]),
            jnp.stack(win_p[2]), jnp.stack(win_s[2]),
            jnp.stack(sbk_p), jnp.stack(sbv_p), jnp.stack(sbk_s), jnp.stack(sbv_s))
```

```python
import functools

import jax
import jax.numpy as jnp
from jax import lax
from jax.experimental import pallas as pl
from jax.experimental.pallas import tpu as pltpu

F32 = jnp.float32
BF16 = jnp.bfloat16

N_HEADS = 16
HEAD_DIM = 128
ATTN_WIDTH = N_HEADS * HEAD_DIM
ATTN_SCALE = HEAD_DIM ** -0.5
DILATED_GROUPS = ((128, 1), (512, 4), (2048, 16))
N_GROUPS = len(DILATED_GROUPS)
N_MIXERS = 2
BLOCK = 128
PAGE_SIZE = 128
ROPE_THETA = 10000.0
RMS_EPS = 1e-6

V7X_VMEM_LIMIT_BYTES = 56 * 1024 * 1024
NEG_INF = float("-inf")


def _params(n_axes):
    return pltpu.CompilerParams(
        dimension_semantics=("arbitrary",) * n_axes,
        vmem_limit_bytes=V7X_VMEM_LIMIT_BYTES,
    )


def _nt_dot(a, b):
    return lax.dot_general(a, b, (((1,), (1,)), ((), ())), preferred_element_type=F32)


def _rmsnorm_kernel(x_ref, g_ref, o_ref):
    x = x_ref[...]
    ms = jnp.mean(x * x, axis=-1, keepdims=True)
    o_ref[...] = (x * lax.rsqrt(ms + RMS_EPS) * g_ref[...]).astype(o_ref.dtype)


def rmsnorm(x, g, out_dtype, tm):
    m, d = x.shape
    return pl.pallas_call(
        _rmsnorm_kernel,
        grid=(m // tm,),
        in_specs=[pl.BlockSpec((tm, d), lambda i: (i, 0)),
                  pl.BlockSpec((1, d), lambda i: (0, 0))],
        out_specs=pl.BlockSpec((tm, d), lambda i: (i, 0)),
        out_shape=jax.ShapeDtypeStruct((m, d), out_dtype),
        compiler_params=_params(1),
        name="rmsnorm",
    )(x, g.reshape(1, d))


def _gateup_kernel(h_ref, wg_ref, wu_ref, o_ref, wg_bf, wu_bf):
    @pl.when(pl.program_id(1) == 0)
    def _():
        wg_bf[...] = wg_ref[...].astype(BF16)
        wu_bf[...] = wu_ref[...].astype(BF16)

    h = h_ref[...]
    g = jnp.dot(h, wg_bf[...], preferred_element_type=F32)
    u = jnp.dot(h, wu_bf[...], preferred_element_type=F32)
    o_ref[...] = (g * jax.nn.sigmoid(g) * u).astype(o_ref.dtype)


def gateup(h, w_gu, layer, tm, tn):
    m, k = h.shape
    f = w_gu.shape[2] // 2
    nf = f // tn
    return pl.pallas_call(
        _gateup_kernel,
        grid=(nf, m // tm),
        in_specs=[pl.BlockSpec((tm, k), lambda j, i: (i, 0)),
                  pl.BlockSpec((None, k, tn), lambda j, i: (layer, 0, j)),
                  pl.BlockSpec((None, k, tn), lambda j, i: (layer, 0, j + nf))],
        out_specs=pl.BlockSpec((tm, tn), lambda j, i: (i, j)),
        out_shape=jax.ShapeDtypeStruct((m, f), BF16),
        scratch_shapes=[pltpu.VMEM((k, tn), BF16), pltpu.VMEM((k, tn), BF16)],
        compiler_params=_params(2),
        name="ffn_gateup",
    )(h, w_gu, w_gu)


def _mm_resid_kernel(a_ref, w_ref, r_ref, o_ref, w_bf, *, scale):
    @pl.when(pl.program_id(1) == 0)
    def _():
        w_bf[...] = w_ref[...].astype(BF16)

    acc = jnp.dot(a_ref[...], w_bf[...], preferred_element_type=F32)
    o_ref[...] = r_ref[...] + scale * acc


def mm_resid(a, w, layer, r, scale, tm, tn):
    m, k = a.shape
    n = w.shape[2]
    return pl.pallas_call(
        functools.partial(_mm_resid_kernel, scale=scale),
        grid=(n // tn, m // tm),
        in_specs=[pl.BlockSpec((tm, k), lambda j, i: (i, 0)),
                  pl.BlockSpec((None, k, tn), lambda j, i: (layer, 0, j)),
                  pl.BlockSpec((tm, tn), lambda j, i: (i, j))],
        out_specs=pl.BlockSpec((tm, tn), lambda j, i: (i, j)),
        out_shape=jax.ShapeDtypeStruct((m, n), F32),
        scratch_shapes=[pltpu.VMEM((k, tn), BF16)],
        compiler_params=_params(2),
        name="mm_resid",
    )(a, w, r)


def _proj_kernel(h_ref, w_ref, cos_ref, sin_ref, o_ref, w_bf, *, rope, tn):
    @pl.when(pl.program_id(1) == 0)
    def _():
        w_bf[...] = w_ref[...].astype(BF16)

    acc = jnp.dot(h_ref[...], w_bf[...], preferred_element_type=F32)
    if not rope:
        o_ref[...] = acc
        return

    kind = (pl.program_id(0) * tn // ATTN_WIDTH) % 3

    @pl.when(kind < 2)
    def _():
        c = cos_ref[...]
        s = sin_ref[...]
        for hh in range(tn // HEAD_DIM):
            x = acc[:, hh * HEAD_DIM:(hh + 1) * HEAD_DIM]
            o_ref[:, hh * HEAD_DIM:(hh + 1) * HEAD_DIM] = (
                x * c + pltpu.roll(x, HEAD_DIM // 2, 1) * s)

    @pl.when(kind == 2)
    def _():
        o_ref[...] = acc


def proj(h, w, layer, cos2, sin2, rope, tm, tn):
    m, k = h.shape
    n = w.shape[2]
    return pl.pallas_call(
        functools.partial(_proj_kernel, rope=rope, tn=tn),
        grid=(n // tn, m // tm),
        in_specs=[pl.BlockSpec((tm, k), lambda j, i: (i, 0)),
                  pl.BlockSpec((None, k, tn), lambda j, i: (layer, 0, j)),
                  pl.BlockSpec((tm, HEAD_DIM), lambda j, i: (i, 0)),
                  pl.BlockSpec((tm, HEAD_DIM), lambda j, i: (i, 0))],
        out_specs=pl.BlockSpec((tm, tn), lambda j, i: (i, j)),
        out_shape=jax.ShapeDtypeStruct((m, n), F32),
        scratch_shapes=[pltpu.VMEM((k, tn), BF16)],
        compiler_params=_params(2),
        name="proj_rope" if rope else "proj",
    )(h, w, cos2, sin2)


BAND_CHUNK = BLOCK * max(d for _, d in DILATED_GROUPS)


def _band_block(q, kc, vc, kp, vp, prev_ok, mask_cur, mask_prev):
    sc = jnp.where(mask_cur, _nt_dot(q, kc) * ATTN_SCALE, NEG_INF)
    mp = mask_prev if prev_ok is None else jnp.logical_and(mask_prev, prev_ok)
    sp = jnp.where(mp, _nt_dot(q, kp) * ATTN_SCALE, NEG_INF)
    m = jnp.maximum(jnp.max(sc, axis=-1, keepdims=True), jnp.max(sp, axis=-1, keepdims=True))
    pc = jnp.exp(sc - m)
    pp = jnp.exp(sp - m)
    l = jnp.sum(pc, axis=-1, keepdims=True) + jnp.sum(pp, axis=-1, keepdims=True)
    acc = (jnp.dot(pc.astype(BF16), vc, preferred_element_type=F32)
           + jnp.dot(pp.astype(BF16), vp, preferred_element_type=F32))
    return acc / l, m + jnp.log(l)


def _band_kernel(*refs):
    ins = refs[:5 * N_GROUPS]
    o_ref = refs[5 * N_GROUPS]
    og = refs[5 * N_GROUPS + 1: 5 * N_GROUPS + 1 + N_GROUPS]
    lg = refs[5 * N_GROUPS + 1 + N_GROUPS:]
    first_chunk = pl.program_id(1) == 0
    row = lax.broadcasted_iota(jnp.int32, (BLOCK, BLOCK), 0)
    col = lax.broadcasted_iota(jnp.int32, (BLOCK, BLOCK), 1)
    mask_cur = col <= row
    mask_prev = col >= row

    for g, (_, dil) in enumerate(DILATED_GROUPS):
        q_ref, kc_ref, vc_ref, kp_ref, vp_ref = ins[5 * g: 5 * g + 5]
        span = BLOCK * dil
        for s in range(BAND_CHUNK // span):
            for c in range(dil):
                def rows(start):
                    if dil == 1:
                        return pl.ds(start, BLOCK)
                    return pl.ds(start, BLOCK, stride=dil)
                cur = rows(s * span + c)
                q = q_ref[cur, :].astype(BF16)
                kc = kc_ref[cur, :].astype(BF16)
                vc = vc_ref[cur, :].astype(BF16)
                if s > 0:
                    prev = rows((s - 1) * span + c)
                    kp = kc_ref[prev, :].astype(BF16)
                    vp = vc_ref[prev, :].astype(BF16)
                    prev_ok = None
                else:
                    prev = rows(c)
                    kp = kp_ref[prev, :].astype(BF16)
                    vp = vp_ref[prev, :].astype(BF16)
                    prev_ok = jnp.logical_not(first_chunk)
                o, lse = _band_block(q, kc, vc, kp, vp, prev_ok, mask_cur, mask_prev)
                og[g][cur, :] = o
                lg[g][cur, :] = jnp.broadcast_to(lse, (BLOCK, HEAD_DIM))

    rows_per = 64

    def merge(r, carry):
        sl = pl.ds(pl.multiple_of(r * rows_per, rows_per), rows_per)
        ls = [lg[g][sl, :] for g in range(N_GROUPS)]
        mx = functools.reduce(jnp.maximum, ls)
        es = [jnp.exp(l - mx) for l in ls]
        den = functools.reduce(lambda a, b: a + b, es)
        num = functools.reduce(lambda a, b: a + b,
                               [es[g] * og[g][sl, :] for g in range(N_GROUPS)])
        o_ref[sl, :] = (num / den).astype(o_ref.dtype)
        return carry

    lax.fori_loop(0, BAND_CHUNK // rows_per, merge, 0)


def band_attention(p, seq):
    def colblk(g, kind, h):
        return (g * 3 + kind) * N_HEADS + h

    in_specs = []
    operands = []
    for g, (_, dil) in enumerate(DILATED_GROUPS):
        span = BLOCK * dil
        per = BAND_CHUNK // span
        for kind in range(3):
            in_specs.append(pl.BlockSpec(
                (BAND_CHUNK, HEAD_DIM), lambda h, n, g=g, kind=kind: (n, colblk(g, kind, h))))
            operands.append(p)
        for kind in (1, 2):
            in_specs.append(pl.BlockSpec(
                (span, HEAD_DIM),
                lambda h, n, g=g, kind=kind, per=per: (jnp.maximum(n * per - 1, 0),
                                                       colblk(g, kind, h))))
            operands.append(p)
    return pl.pallas_call(
        _band_kernel,
        grid=(N_HEADS, seq // BAND_CHUNK),
        in_specs=in_specs,
        out_specs=pl.BlockSpec((BAND_CHUNK, HEAD_DIM), lambda h, n: (n, h)),
        out_shape=jax.ShapeDtypeStruct((seq, ATTN_WIDTH), BF16),
        scratch_shapes=[pltpu.VMEM((BAND_CHUNK, HEAD_DIM), F32)] * (2 * N_GROUPS),
        compiler_params=_params(2),
        name="band_attention",
    )(*operands)


def _split_heads_kernel(*refs):
    o_ref = refs[-1]
    for part, x_ref in enumerate(refs[:-1]):
        for h in range(o_ref.shape[2]):
            o_ref[:, part, h, :] = x_ref[:, h * HEAD_DIM:(h + 1) * HEAD_DIM]


def split_heads(p, row0, n_rows, col0s, n_heads):
    width = n_heads * HEAD_DIM
    tr = min(n_rows, 512)
    assert row0 % tr == 0 and n_rows % tr == 0 and all(c % width == 0 for c in col0s)
    return pl.pallas_call(
        _split_heads_kernel,
        grid=(n_rows // tr,),
        in_specs=[pl.BlockSpec((tr, width), lambda i, c=c: (row0 // tr + i, c // width))
                  for c in col0s],
        out_specs=pl.BlockSpec((tr, len(col0s), n_heads, HEAD_DIM), lambda i: (i, 0, 0, 0)),
        out_shape=jax.ShapeDtypeStruct((n_rows, len(col0s), n_heads, HEAD_DIM), F32),
        compiler_params=_params(1),
        name="split_heads",
    )(*([p] * len(col0s)))


def _shift_states_kernel(*refs, n_new):
    ps_ref = refs[0]
    bufs = refs[1:1 + N_GROUPS]
    outs = refs[1 + N_GROUPS:1 + 2 * N_GROUPS]
    sem = refs[-1]
    b = pl.program_id(0)
    copies = []
    for g, (buf, out) in enumerate(zip(bufs, outs)):
        keep = buf.shape[1] - n_new
        copies.append(pltpu.make_async_copy(
            buf.at[b, pl.ds(n_new, keep)], out.at[b, pl.ds(0, keep)], sem.at[2 * g]))
        copies.append(pltpu.make_async_copy(
            ps_ref.at[b, :, g, pl.ds(1, 2)], out.at[b, pl.ds(keep, n_new)], sem.at[2 * g + 1]))
    for c in copies:
        c.start()
    for c in copies:
        c.wait()


def shift_states(ps, bufs):
    nb, n_new = ps.shape[:2]
    any_spec = pl.BlockSpec(memory_space=pl.ANY)
    return pl.pallas_call(
        functools.partial(_shift_states_kernel, n_new=n_new),
        grid=(nb,),
        in_specs=[any_spec] * (1 + N_GROUPS),
        out_specs=[any_spec] * N_GROUPS,
        out_shape=[jax.ShapeDtypeStruct(buf.shape, buf.dtype) for buf in bufs],
        scratch_shapes=[pltpu.SemaphoreType.DMA((2 * N_GROUPS,))],
        compiler_params=_params(1),
        name="shift_states",
    )(ps, *bufs)


def _window_kernel(ps_ref, b0_ref, b1_ref, b2_ref, o_ref, *, n_new):
    t = pl.program_id(1)
    key_i = lax.broadcasted_iota(jnp.int32, (BLOCK, N_HEADS, 1), 0)
    outs, lses = [], []
    for g, ((_, dil), buf_ref) in enumerate(zip(DILATED_GROUPS, (b0_ref, b1_ref, b2_ref))):
        h0 = g * 3 * N_HEADS
        q = ps_ref[0, t, h0: h0 + N_HEADS, :]
        k = buf_ref[0, :, 0, 0]
        v = buf_ref[0, :, 0, 1]
        s_past = jnp.sum(k * q[None], axis=-1, keepdims=True) * ATTN_SCALE
        if dil == 1:
            s_past = jnp.where(key_i >= t, s_past, NEG_INF)
        m = jnp.max(s_past, axis=0)
        s_new = []
        for r in range(n_new):
            kn = ps_ref[0, r, h0 + N_HEADS: h0 + 2 * N_HEADS, :]
            sr = jnp.sum(q * kn, axis=-1, keepdims=True) * ATTN_SCALE
            ok = (r <= t) if dil == 1 else (r == t)
            sr = jnp.where(ok, sr, NEG_INF)
            s_new.append(sr)
            m = jnp.maximum(m, sr)
        p_past = jnp.exp(s_past - m[None])
        l = jnp.sum(p_past, axis=0)
        o = jnp.sum(p_past * v, axis=0)
        for r in range(n_new):
            pr = jnp.exp(s_new[r] - m)
            l = l + pr
            o = o + pr * ps_ref[0, r, h0 + 2 * N_HEADS: h0 + 3 * N_HEADS, :]
        outs.append(o / l)
        lses.append(m + jnp.log(l))
    mx = functools.reduce(jnp.maximum, lses)
    es = [jnp.exp(l - mx) for l in lses]
    den = functools.reduce(lambda a, b: a + b, es)
    num = functools.reduce(lambda a, b: a + b, [e * o for e, o in zip(es, outs)])
    merged = num / den
    for h in range(N_HEADS):
        o_ref[0, 0, :, h * HEAD_DIM:(h + 1) * HEAD_DIM] = merged[h:h + 1, :]


def window_attention(ps, bufs):
    nb, n_new = ps.shape[:2]
    in_specs = [pl.BlockSpec((1, n_new) + ps.shape[2:], lambda b, t: (b, 0, 0, 0))]
    operands = [ps]
    for (window, dil), buf in zip(DILATED_GROUPS, bufs):
        assert buf.shape[1] == window == BLOCK * dil
        operands.append(buf.reshape(nb, BLOCK, dil, 2, N_HEADS, HEAD_DIM))
        blk = (1, BLOCK, 1, 2, N_HEADS, HEAD_DIM)
        if dil == 1:
            in_specs.append(pl.BlockSpec(blk, lambda b, t: (b, 0, 0, 0, 0, 0)))
        else:
            in_specs.append(pl.BlockSpec(blk, lambda b, t: (b, 0, t, 0, 0, 0)))
    out = pl.pallas_call(
        functools.partial(_window_kernel, n_new=n_new),
        grid=(nb, n_new),
        in_specs=in_specs,
        out_specs=pl.BlockSpec((1, 1, 1, ATTN_WIDTH), lambda b, t: (b, t, 0, 0)),
        out_shape=jax.ShapeDtypeStruct((nb, n_new, 1, ATTN_WIDTH), F32),
        compiler_params=_params(2),
        name="window_attention",
    )(*operands)
    return out.reshape(nb * n_new, ATTN_WIDTH)


SB_TQ = 512
SB_TK = 256


def _suffix_matrix(n):
    r = lax.broadcasted_iota(jnp.int32, (n, n), 0)
    c = lax.broadcasted_iota(jnp.int32, (n, n), 1)
    return jnp.where(r > c, 1.0, 0.0).astype(BF16)


def _stick_tile(z, causal, carry, tri):
    soft = jnp.log1p(jnp.exp(-jnp.abs(z)))
    u = -(jnp.maximum(z, 0.0) + soft)
    log_sig = jnp.minimum(z, 0.0) - soft
    if causal is not None:
        u = jnp.where(causal, u, 0.0)
    u_hi = u.astype(BF16)
    u_lo = (u - u_hi.astype(F32)).astype(BF16)
    after = (jnp.dot(u_hi, tri, preferred_element_type=F32)
             + jnp.dot(u_lo, tri, preferred_element_type=F32))
    a = jnp.exp(log_sig + after + carry)
    if causal is not None:
        a = jnp.where(causal, a, 0.0)
    return a, carry + after[:, :1] + u[:, :1]


def _sb_prompt_kernel(bias_ref, q_ref, k_ref, v_ref, o_ref, k_bf, v_bf):
    h = pl.program_id(0)
    i = pl.program_id(1)

    @pl.when(i == 0)
    def _():
        k_bf[...] = k_ref[...].astype(BF16)
        v_bf[...] = v_ref[...].astype(BF16)

    q = q_ref[...].astype(BF16)
    bias = bias_ref[h]
    tri = _suffix_matrix(SB_TK)
    q_pos = i * SB_TQ + lax.broadcasted_iota(jnp.int32, (SB_TQ, SB_TK), 0)
    k_off = lax.broadcasted_iota(jnp.int32, (SB_TQ, SB_TK), 1)
    n_tiles = (i + 1) * (SB_TQ // SB_TK)

    def body(jj, state):
        carry, acc = state
        j = n_tiles - 1 - jj
        rows = pl.ds(pl.multiple_of(j * SB_TK, SB_TK), SB_TK)
        z = _nt_dot(q, k_bf[rows, :]) * ATTN_SCALE + bias
        causal = (j * SB_TK + k_off) < q_pos
        a, carry = _stick_tile(z, causal, carry, tri)
        acc = acc + jnp.dot(a.astype(BF16), v_bf[rows, :], preferred_element_type=F32)
        return carry, acc

    init = (jnp.zeros((SB_TQ, 1), F32), jnp.zeros((SB_TQ, HEAD_DIM), F32))
    _, acc = lax.fori_loop(0, n_tiles, body, init)
    o_ref[...] = acc.astype(o_ref.dtype)


def sb_prompt_attention(p, bias, seq):
    return pl.pallas_call(
        _sb_prompt_kernel,
        grid=(N_HEADS, seq // SB_TQ),
        in_specs=[pl.BlockSpec(memory_space=pltpu.SMEM),
                  pl.BlockSpec((SB_TQ, HEAD_DIM), lambda h, i: (i, h)),
                  pl.BlockSpec((seq, HEAD_DIM), lambda h, i: (0, N_HEADS + h)),
                  pl.BlockSpec((seq, HEAD_DIM), lambda h, i: (0, 2 * N_HEADS + h))],
        out_specs=pl.BlockSpec((SB_TQ, HEAD_DIM), lambda h, i: (i, h)),
        out_shape=jax.ShapeDtypeStruct((seq, ATTN_WIDTH), BF16),
        scratch_shapes=[pltpu.VMEM((seq, HEAD_DIM), BF16), pltpu.VMEM((seq, HEAD_DIM), BF16)],
        compiler_params=_params(2),
        name="sb_prompt",
    )(bias, p, p, p)


SB_ROWS = 16


def _sb_sample_kernel(pt_ref, ps_ref, bias_ref, kc_ref, vc_ref, o_ref,
                      q_bf, acc_ref, carry_ref, qf_ref, kn_ref, vn_ref, *, n_new):
    del pt_ref
    p = pl.program_id(1)
    tri = _suffix_matrix(PAGE_SIZE)
    bias = bias_ref[...]

    def tile(k_ref, v_ref, causal):
        s = jnp.concatenate(
            [_nt_dot(q_bf[h], k_ref[:, h, :].astype(BF16)) for h in range(N_HEADS)], axis=0)
        z = s * ATTN_SCALE + bias
        a, carry = _stick_tile(z, causal, carry_ref[...], tri)
        carry_ref[...] = carry
        a = a.astype(BF16)
        for h in range(N_HEADS):
            acc_ref[h] += jnp.dot(a[h * SB_ROWS:(h + 1) * SB_ROWS], v_ref[:, h, :].astype(BF16),
                                  preferred_element_type=F32)

    @pl.when(p == 0)
    def _():
        qf_ref[...] = jnp.zeros_like(qf_ref)
        kn_ref[...] = jnp.zeros_like(kn_ref)
        vn_ref[...] = jnp.zeros_like(vn_ref)
        for r in range(n_new):
            qf_ref[r] = ps_ref[0, r, 0:N_HEADS, :]
            kn_ref[r] = ps_ref[0, r, N_HEADS:2 * N_HEADS, :]
            vn_ref[r] = ps_ref[0, r, 2 * N_HEADS:3 * N_HEADS, :]
        for h in range(N_HEADS):
            q_bf[h] = qf_ref[:, h, :].astype(BF16)
        acc_ref[...] = jnp.zeros_like(acc_ref)
        carry_ref[...] = jnp.zeros_like(carry_ref)
        shape = (N_HEADS * SB_ROWS, PAGE_SIZE)
        step = lax.broadcasted_iota(jnp.int32, shape, 0) % SB_ROWS
        key = lax.broadcasted_iota(jnp.int32, shape, 1)
        tile(kn_ref, vn_ref, key < step)

    tile(kc_ref.at[0, 0], vc_ref.at[0, 0], None)

    @pl.when(p == pl.num_programs(1) - 1)
    def _():
        for r in range(n_new):
            for h in range(N_HEADS):
                o_ref[0, r, :, h * HEAD_DIM:(h + 1) * HEAD_DIM] = acc_ref[h, r:r + 1, :]


def sb_sample_attention(ps, cache_k, cache_v, layer, page_table, bias):
    nb, n_new = ps.shape[:2]
    n_pages = page_table.shape[1]
    n_rows = N_HEADS * SB_ROWS
    assert n_new <= SB_ROWS

    def page(b, p, pt):
        return (layer, pt[b * n_pages + (n_pages - 1 - p)], 0, 0, 0)

    page_block = (1, 1, PAGE_SIZE, N_HEADS, HEAD_DIM)
    grid_spec = pltpu.PrefetchScalarGridSpec(
        num_scalar_prefetch=1,
        grid=(nb, n_pages),
        in_specs=[pl.BlockSpec((1, n_new) + ps.shape[2:], lambda b, p, pt: (b, 0, 0, 0)),
                  pl.BlockSpec((n_rows, 1), lambda b, p, pt: (0, 0)),
                  pl.BlockSpec(page_block, page),
                  pl.BlockSpec(page_block, page)],
        out_specs=pl.BlockSpec((1, n_new, 1, ATTN_WIDTH), lambda b, p, pt: (b, 0, 0, 0)),
        scratch_shapes=[pltpu.VMEM((N_HEADS, SB_ROWS, HEAD_DIM), BF16),
                        pltpu.VMEM((N_HEADS, SB_ROWS, HEAD_DIM), F32),
                        pltpu.VMEM((n_rows, 1), F32),
                        pltpu.VMEM((SB_ROWS, N_HEADS, HEAD_DIM), F32),
                        pltpu.VMEM((PAGE_SIZE, N_HEADS, HEAD_DIM), F32),
                        pltpu.VMEM((PAGE_SIZE, N_HEADS, HEAD_DIM), F32)],
    )
    out = pl.pallas_call(
        functools.partial(_sb_sample_kernel, n_new=n_new),
        grid_spec=grid_spec,
        out_shape=jax.ShapeDtypeStruct((nb, n_new, 1, ATTN_WIDTH), F32),
        compiler_params=_params(2),
        name="sb_sample",
    )(page_table.reshape(-1), ps, jnp.repeat(bias.astype(F32), SB_ROWS).reshape(n_rows, 1),
      cache_k, cache_v)
    return out.reshape(nb * n_new, ATTN_WIDTH)


def _rope_tables(pos):
    half = HEAD_DIM // 2
    inv = ROPE_THETA ** (-jnp.arange(half, dtype=F32) / half)
    ang = pos.astype(F32)[:, None] * inv[None, :]
    cos = jnp.cos(ang)
    sin = jnp.sin(ang)
    return jnp.concatenate([cos, cos], axis=-1), jnp.concatenate([-sin, sin], axis=-1)


def _macaron_half(x, g, w_gu, w_d, layer):
    h = rmsnorm(x, g, BF16, tm=640)
    a = gateup(h, w_gu, layer, tm=832, tn=512)
    return mm_resid(a, w_d, layer, x, 0.5, tm=832, tn=256)


def kernel(x_prompt, x_sample, state_win128_kv, state_win512_kv, state_win2048_kv, cache_sb_k, cache_sb_v, page_table, norm_pre, ffn_pre_wgu, ffn_pre_wd, norm_mix, a_w_in, a_w_out, b_w_qkv, b_logit_bias, b_w_out, norm_post, ffn_post_wgu, ffn_post_wd, final_norm):
    _, seq, d_model = x_prompt.shape
    nb, n_new, _ = x_sample.shape
    depth = norm_pre.shape[0]
    past_len = page_table.shape[1] * PAGE_SIZE
    win_bufs = (state_win128_kv, state_win512_kv, state_win2048_kv)

    y = jnp.concatenate([x_prompt.reshape(seq, d_model), x_sample.reshape(nb * n_new, d_model)])
    pos = jnp.concatenate([jnp.arange(seq), past_len + jnp.tile(jnp.arange(n_new), nb)])
    cos2, sin2 = _rope_tables(pos)

    win_p = [[] for _ in DILATED_GROUPS]
    win_s = [[] for _ in DILATED_GROUPS]
    sbk_p, sbv_p, sbk_s, sbv_s = [], [], [], []
    for i in range(depth):
        j = i // N_MIXERS
        y = _macaron_half(y, norm_pre[i], ffn_pre_wgu, ffn_pre_wd, i)
        h = rmsnorm(y, norm_mix[i], BF16, tm=640)
        if i % N_MIXERS == 0:
            p = proj(h, a_w_in, j, cos2, sin2, True, tm=1664, tn=512)
            n_slots = p.shape[1] // HEAD_DIM
            ps = split_heads(p, seq, nb * n_new, (0,), n_slots)
            bufs = tuple(buf[j] for buf in win_bufs)
            for (window, _), buf in zip(DILATED_GROUPS, bufs):
                assert buf.shape[1] == window and seq >= window
            o_s = window_attention(ps.reshape(nb, n_new, n_slots, HEAD_DIM), bufs)
            o = jnp.concatenate([band_attention(p, seq), o_s.astype(BF16)])
            y = mm_resid(o, a_w_out, j, y, 1.0, tm=1664, tn=512)
            new_states = shift_states(
                ps.reshape(nb, n_new, N_GROUPS, 3, N_HEADS, HEAD_DIM), bufs)
            for g, (window, _) in enumerate(DILATED_GROUPS):
                cols = tuple((g * 3 + kind) * ATTN_WIDTH for kind in (1, 2))
                win_p[g].append(split_heads(p, seq - window, window, cols, N_HEADS)[None])
                win_s[g].append(new_states[g])
        else:
            p = proj(h, b_w_qkv, j, cos2, sin2, False, tm=1664, tn=512)
            ps = split_heads(p, seq, nb * n_new, (0,), 3 * N_HEADS).reshape(
                nb, n_new, 3 * N_HEADS, HEAD_DIM)
            o_s = sb_sample_attention(ps, cache_sb_k, cache_sb_v, j, page_table, b_logit_bias[j])
            o = jnp.concatenate([sb_prompt_attention(p, b_logit_bias[j].astype(F32), seq),
                                 o_s.astype(BF16)])
            y = mm_resid(o, b_w_out, j, y, 1.0, tm=1664, tn=512)
            sbk_p.append(split_heads(p, 0, seq, (ATTN_WIDTH,), N_HEADS).reshape(
                1, seq, N_HEADS, HEAD_DIM))
            sbv_p.append(split_heads(p, 0, seq, (2 * ATTN_WIDTH,), N_HEADS).reshape(
                1, seq, N_HEADS, HEAD_DIM))
            sbk_s.append(ps[:, :, N_HEADS:2 * N_HEADS])
            sbv_s.append(ps[:, :, 2 * N_HEADS:])
        y = _macaron_half(y, norm_post[i], ffn_post_wgu, ffn_post_wd, i)
    y = rmsnorm(y, final_norm, F32, tm=640)
    return (y[:seq].reshape(1, seq, d_model), y[seq:].reshape(nb, n_new, d_model),
            jnp.stack(win_p[0]), jnp.stack(win_s[0]),
            jnp.stack(win_p[1]), jnp.stack(win_s[1]),
            jnp.stack(win_p[2]), jnp.stack(win_s[2]),
            jnp.stack(sbk_p), jnp.stack(sbv_p), jnp.stack(sbk_s), jnp.stack(sbv_s))
```

```python
import functools

import jax
import jax.numpy as jnp
from jax import lax
from jax.experimental import pallas as pl
from jax.experimental.pallas import tpu as pltpu

F32 = jnp.float32
BF16 = jnp.bfloat16

N_HEADS = 16
HEAD_DIM = 128
ATTN_WIDTH = N_HEADS * HEAD_DIM
ATTN_SCALE = HEAD_DIM ** -0.5
DILATED_GROUPS = ((128, 1), (512, 4), (2048, 16))
N_GROUPS = len(DILATED_GROUPS)
N_MIXERS = 2
BLOCK = 128
PAGE_SIZE = 128
ROPE_THETA = 10000.0
RMS_EPS = 1e-6

V7X_VMEM_LIMIT_BYTES = 56 * 1024 * 1024
NEG_INF = float("-inf")


def _params(n_axes):
    return pltpu.CompilerParams(
        dimension_semantics=("arbitrary",) * n_axes,
        vmem_limit_bytes=V7X_VMEM_LIMIT_BYTES,
    )


def _nt_dot(a, b):
    return lax.dot_general(a, b, (((1,), (1,)), ((), ())), preferred_element_type=F32)


def _rmsnorm_kernel(x_ref, g_ref, o_ref):
    x = x_ref[...]
    ms = jnp.mean(x * x, axis=-1, keepdims=True)
    o_ref[...] = (x * lax.rsqrt(ms + RMS_EPS) * g_ref[...]).astype(o_ref.dtype)


def rmsnorm(x, g, out_dtype, tm):
    m, d = x.shape
    return pl.pallas_call(
        _rmsnorm_kernel,
        grid=(m // tm,),
        in_specs=[pl.BlockSpec((tm, d), lambda i: (i, 0)),
                  pl.BlockSpec((1, d), lambda i: (0, 0))],
        out_specs=pl.BlockSpec((tm, d), lambda i: (i, 0)),
        out_shape=jax.ShapeDtypeStruct((m, d), out_dtype),
        compiler_params=_params(1),
        name="rmsnorm",
    )(x, g.reshape(1, d))


def _gateup_kernel(h_ref, wg_ref, wu_ref, o_ref, wg_bf, wu_bf):
    @pl.when(pl.program_id(1) == 0)
    def _():
        wg_bf[...] = wg_ref[...].astype(BF16)
        wu_bf[...] = wu_ref[...].astype(BF16)

    h = h_ref[...]
    g = jnp.dot(h, wg_bf[...], preferred_element_type=F32)
    u = jnp.dot(h, wu_bf[...], preferred_element_type=F32)
    o_ref[...] = (g * jax.nn.sigmoid(g) * u).astype(o_ref.dtype)


def gateup(h, w_gu, layer, tm, tn):
    m, k = h.shape
    f = w_gu.shape[2] // 2
    nf = f // tn
    return pl.pallas_call(
        _gateup_kernel,
        grid=(nf, m // tm),
        in_specs=[pl.BlockSpec((tm, k), lambda j, i: (i, 0)),
                  pl.BlockSpec((None, k, tn), lambda j, i: (layer, 0, j)),
                  pl.BlockSpec((None, k, tn), lambda j, i: (layer, 0, j + nf))],
        out_specs=pl.BlockSpec((tm, tn), lambda j, i: (i, j)),
        out_shape=jax.ShapeDtypeStruct((m, f), BF16),
        scratch_shapes=[pltpu.VMEM((k, tn), BF16), pltpu.VMEM((k, tn), BF16)],
        compiler_params=_params(2),
        name="ffn_gateup",
    )(h, w_gu, w_gu)


def _mm_resid_kernel(a_ref, w_ref, r_ref, o_ref, w_bf, *, scale):
    @pl.when(pl.program_id(1) == 0)
    def _():
        w_bf[...] = w_ref[...].astype(BF16)

    acc = jnp.dot(a_ref[...], w_bf[...], preferred_element_type=F32)
    o_ref[...] = r_ref[...] + scale * acc


def mm_resid(a, w, layer, r, scale, tm, tn):
    m, k = a.shape
    n = w.shape[2]
    return pl.pallas_call(
        functools.partial(_mm_resid_kernel, scale=scale),
        grid=(n // tn, m // tm),
        in_specs=[pl.BlockSpec((tm, k), lambda j, i: (i, 0)),
                  pl.BlockSpec((None, k, tn), lambda j, i: (layer, 0, j)),
                  pl.BlockSpec((tm, tn), lambda j, i: (i, j))],
        out_specs=pl.BlockSpec((tm, tn), lambda j, i: (i, j)),
        out_shape=jax.ShapeDtypeStruct((m, n), F32),
        scratch_shapes=[pltpu.VMEM((k, tn), BF16)],
        compiler_params=_params(2),
        name="mm_resid",
    )(a, w, r)


def _proj_kernel(h_ref, w_ref, cos_ref, sin_ref, o_ref, w_bf, *, rope, tn):
    @pl.when(pl.program_id(1) == 0)
    def _():
        w_bf[...] = w_ref[...].astype(BF16)

    acc = jnp.dot(h_ref[...], w_bf[...], preferred_element_type=F32)
    if not rope:
        o_ref[...] = acc
        return

    kind = (pl.program_id(0) * tn // ATTN_WIDTH) % 3

    @pl.when(kind < 2)
    def _():
        c = cos_ref[...]
        s = sin_ref[...]
        for hh in range(tn // HEAD_DIM):
            x = acc[:, hh * HEAD_DIM:(hh + 1) * HEAD_DIM]
            o_ref[:, hh * HEAD_DIM:(hh + 1) * HEAD_DIM] = (
                x * c + pltpu.roll(x, HEAD_DIM // 2, 1) * s)

    @pl.when(kind == 2)
    def _():
        o_ref[...] = acc


def proj(h, w, layer, cos2, sin2, rope, tm, tn):
    m, k = h.shape
    n = w.shape[2]
    return pl.pallas_call(
        functools.partial(_proj_kernel, rope=rope, tn=tn),
        grid=(n // tn, m // tm),
        in_specs=[pl.BlockSpec((tm, k), lambda j, i: (i, 0)),
                  pl.BlockSpec((None, k, tn), lambda j, i: (layer, 0, j)),
                  pl.BlockSpec((tm, HEAD_DIM), lambda j, i: (i, 0)),
                  pl.BlockSpec((tm, HEAD_DIM), lambda j, i: (i, 0))],
        out_specs=pl.BlockSpec((tm, tn), lambda j, i: (i, j)),
        out_shape=jax.ShapeDtypeStruct((m, n), F32),
        scratch_shapes=[pltpu.VMEM((k, tn), BF16)],
        compiler_params=_params(2),
        name="proj_rope" if rope else "proj",
    )(h, w, cos2, sin2)


BAND_CHUNK = BLOCK * max(d for _, d in DILATED_GROUPS)


def _band_block(q, kc, vc, kp, vp, prev_ok, mask_cur, mask_prev):
    sc = jnp.where(mask_cur, _nt_dot(q, kc) * ATTN_SCALE, NEG_INF)
    mp = mask_prev if prev_ok is None else jnp.logical_and(mask_prev, prev_ok)
    sp = jnp.where(mp, _nt_dot(q, kp) * ATTN_SCALE, NEG_INF)
    m = jnp.maximum(jnp.max(sc, axis=-1, keepdims=True), jnp.max(sp, axis=-1, keepdims=True))
    pc = jnp.exp(sc - m)
    pp = jnp.exp(sp - m)
    l = jnp.sum(pc, axis=-1, keepdims=True) + jnp.sum(pp, axis=-1, keepdims=True)
    acc = (jnp.dot(pc.astype(BF16), vc, preferred_element_type=F32)
           + jnp.dot(pp.astype(BF16), vp, preferred_element_type=F32))
    return acc / l, m + jnp.log(l)


def _band_kernel(*refs):
    ins = refs[:5 * N_GROUPS]
    o_ref = refs[5 * N_GROUPS]
    og = refs[5 * N_GROUPS + 1: 5 * N_GROUPS + 1 + N_GROUPS]
    lg = refs[5 * N_GROUPS + 1 + N_GROUPS:]
    first_chunk = pl.program_id(1) == 0
    row = lax.broadcasted_iota(jnp.int32, (BLOCK, BLOCK), 0)
    col = lax.broadcasted_iota(jnp.int32, (BLOCK, BLOCK), 1)
    mask_cur = col <= row
    mask_prev = col >= row

    for g, (_, dil) in enumerate(DILATED_GROUPS):
        q_ref, kc_ref, vc_ref, kp_ref, vp_ref = ins[5 * g: 5 * g + 5]
        span = BLOCK * dil
        for s in range(BAND_CHUNK // span):
            for c in range(dil):
                def rows(start):
                    if dil == 1:
                        return pl.ds(start, BLOCK)
                    return pl.ds(start, BLOCK, stride=dil)
                cur = rows(s * span + c)
                q = q_ref[cur, :].astype(BF16)
                kc = kc_ref[cur, :].astype(BF16)
                vc = vc_ref[cur, :].astype(BF16)
                if s > 0:
                    prev = rows((s - 1) * span + c)
                    kp = kc_ref[prev, :].astype(BF16)
                    vp = vc_ref[prev, :].astype(BF16)
                    prev_ok = None
                else:
                    prev = rows(c)
                    kp = kp_ref[prev, :].astype(BF16)
                    vp = vp_ref[prev, :].astype(BF16)
                    prev_ok = jnp.logical_not(first_chunk)
                o, lse = _band_block(q, kc, vc, kp, vp, prev_ok, mask_cur, mask_prev)
                og[g][cur, :] = o
                lg[g][cur, :] = jnp.broadcast_to(lse, (BLOCK, HEAD_DIM))

    rows_per = 64

    def merge(r, carry):
        sl = pl.ds(pl.multiple_of(r * rows_per, rows_per), rows_per)
        ls = [lg[g][sl, :] for g in range(N_GROUPS)]
        mx = functools.reduce(jnp.maximum, ls)
        es = [jnp.exp(l - mx) for l in ls]
        den = functools.reduce(lambda a, b: a + b, es)
        num = functools.reduce(lambda a, b: a + b,
                               [es[g] * og[g][sl, :] for g in range(N_GROUPS)])
        o_ref[sl, :] = (num / den).astype(o_ref.dtype)
        return carry

    lax.fori_loop(0, BAND_CHUNK // rows_per, merge, 0)


def band_attention(p, seq):
    def colblk(g, kind, h):
        return (g * 3 + kind) * N_HEADS + h

    in_specs = []
    operands = []
    for g, (_, dil) in enumerate(DILATED_GROUPS):
        span = BLOCK * dil
        per = BAND_CHUNK // span
        for kind in range(3):
            in_specs.append(pl.BlockSpec(
                (BAND_CHUNK, HEAD_DIM), lambda h, n, g=g, kind=kind: (n, colblk(g, kind, h))))
            operands.append(p)
        for kind in (1, 2):
            in_specs.append(pl.BlockSpec(
                (span, HEAD_DIM),
                lambda h, n, g=g, kind=kind, per=per: (jnp.maximum(n * per - 1, 0),
                                                       colblk(g, kind, h))))
            operands.append(p)
    return pl.pallas_call(
        _band_kernel,
        grid=(N_HEADS, seq // BAND_CHUNK),
        in_specs=in_specs,
        out_specs=pl.BlockSpec((BAND_CHUNK, HEAD_DIM), lambda h, n: (n, h)),
        out_shape=jax.ShapeDtypeStruct((seq, ATTN_WIDTH), BF16),
        scratch_shapes=[pltpu.VMEM((BAND_CHUNK, HEAD_DIM), F32)] * (2 * N_GROUPS),
        compiler_params=_params(2),
        name="band_attention",
    )(*operands)


def _split_heads_kernel(*refs):
    o_ref = refs[-1]
    for part, x_ref in enumerate(refs[:-1]):
        for h in range(o_ref.shape[2]):
            o_ref[:, part, h, :] = x_ref[:, h * HEAD_DIM:(h + 1) * HEAD_DIM]


def split_heads(p, row0, n_rows, col0s, n_heads):
    width = n_heads * HEAD_DIM
    tr = min(n_rows, 512)
    assert row0 % tr == 0 and n_rows % tr == 0 and all(c % width == 0 for c in col0s)
    return pl.pallas_call(
        _split_heads_kernel,
        grid=(n_rows // tr,),
        in_specs=[pl.BlockSpec((tr, width), lambda i, c=c: (row0 // tr + i, c // width))
                  for c in col0s],
        out_specs=pl.BlockSpec((tr, len(col0s), n_heads, HEAD_DIM), lambda i: (i, 0, 0, 0)),
        out_shape=jax.ShapeDtypeStruct((n_rows, len(col0s), n_heads, HEAD_DIM), F32),
        compiler_params=_params(1),
        name="split_heads",
    )(*([p] * len(col0s)))


SHIFT_CHUNK = 128


def _shift_state_kernel(ps_ref, cur_ref, nxt_ref, o_ref, *, n_new):
    chunk = cur_ref.shape[1]
    last = pl.program_id(1) == pl.num_programs(1) - 1
    o_ref[0, 0:chunk - n_new] = cur_ref[0, n_new:chunk]

    @pl.when(last)
    def _():
        o_ref[0, chunk - n_new:chunk] = ps_ref[0, :, 0, 1:3]

    @pl.when(jnp.logical_not(last))
    def _():
        o_ref[0, chunk - n_new:chunk] = nxt_ref[0]


def shift_state(ps, group, buf):
    nb, n_new = ps.shape[:2]
    window = buf.shape[1]
    chunk = min(SHIFT_CHUNK, window)
    n_chunks = window // chunk
    assert window % chunk == 0 and chunk % n_new == 0
    per = chunk // n_new
    tail = buf.shape[2:]
    zeros = (0,) * len(tail)
    return pl.pallas_call(
        functools.partial(_shift_state_kernel, n_new=n_new),
        grid=(nb, n_chunks),
        in_specs=[pl.BlockSpec((1, n_new, 1) + ps.shape[3:], lambda b, c: (b, 0, group, 0, 0, 0)),
                  pl.BlockSpec((1, chunk) + tail, lambda b, c: (b, c) + zeros),
                  pl.BlockSpec((1, n_new) + tail,
                               lambda b, c: (b, jnp.minimum(c + 1, n_chunks - 1) * per) + zeros)],
        out_specs=pl.BlockSpec((1, chunk) + tail, lambda b, c: (b, c) + zeros),
        out_shape=jax.ShapeDtypeStruct(buf.shape, buf.dtype),
        compiler_params=_params(2),
        name="shift_state",
    )(ps, buf, buf)


def _window_kernel(ps_ref, b0_ref, b1_ref, b2_ref, o_ref, *, n_new):
    t = pl.program_id(1)
    key_i = lax.broadcasted_iota(jnp.int32, (BLOCK, N_HEADS, 1), 0)
    outs, lses = [], []
    for g, ((_, dil), buf_ref) in enumerate(zip(DILATED_GROUPS, (b0_ref, b1_ref, b2_ref))):
        h0 = g * 3 * N_HEADS
        q = ps_ref[0, t, h0: h0 + N_HEADS, :]
        k = buf_ref[0, :, 0, 0]
        v = buf_ref[0, :, 0, 1]
        s_past = jnp.sum(k * q[None], axis=-1, keepdims=True) * ATTN_SCALE
        if dil == 1:
            s_past = jnp.where(key_i >= t, s_past, NEG_INF)
        m = jnp.max(s_past, axis=0)
        s_new = []
        for r in range(n_new):
            kn = ps_ref[0, r, h0 + N_HEADS: h0 + 2 * N_HEADS, :]
            sr = jnp.sum(q * kn, axis=-1, keepdims=True) * ATTN_SCALE
            ok = (r <= t) if dil == 1 else (r == t)
            sr = jnp.where(ok, sr, NEG_INF)
            s_new.append(sr)
            m = jnp.maximum(m, sr)
        p_past = jnp.exp(s_past - m[None])
        l = jnp.sum(p_past, axis=0)
        o = jnp.sum(p_past * v, axis=0)
        for r in range(n_new):
            pr = jnp.exp(s_new[r] - m)
            l = l + pr
            o = o + pr * ps_ref[0, r, h0 + 2 * N_HEADS: h0 + 3 * N_HEADS, :]
        outs.append(o / l)
        lses.append(m + jnp.log(l))
    mx = functools.reduce(jnp.maximum, lses)
    es = [jnp.exp(l - mx) for l in lses]
    den = functools.reduce(lambda a, b: a + b, es)
    num = functools.reduce(lambda a, b: a + b, [e * o for e, o in zip(es, outs)])
    merged = num / den
    for h in range(N_HEADS):
        o_ref[0, 0, :, h * HEAD_DIM:(h + 1) * HEAD_DIM] = merged[h:h + 1, :]


def window_attention(ps, bufs):
    nb, n_new = ps.shape[:2]
    in_specs = [pl.BlockSpec((1, n_new) + ps.shape[2:], lambda b, t: (b, 0, 0, 0))]
    operands = [ps]
    for (window, dil), buf in zip(DILATED_GROUPS, bufs):
        assert buf.shape[1] == window == BLOCK * dil
        operands.append(buf.reshape(nb, BLOCK, dil, 2, N_HEADS, HEAD_DIM))
        blk = (1, BLOCK, 1, 2, N_HEADS, HEAD_DIM)
        if dil == 1:
            in_specs.append(pl.BlockSpec(blk, lambda b, t: (b, 0, 0, 0, 0, 0)))
        else:
            in_specs.append(pl.BlockSpec(blk, lambda b, t: (b, 0, t, 0, 0, 0)))
    out = pl.pallas_call(
        functools.partial(_window_kernel, n_new=n_new),
        grid=(nb, n_new),
        in_specs=in_specs,
        out_specs=pl.BlockSpec((1, 1, 1, ATTN_WIDTH), lambda b, t: (b, t, 0, 0)),
        out_shape=jax.ShapeDtypeStruct((nb, n_new, 1, ATTN_WIDTH), F32),
        compiler_params=_params(2),
        name="window_attention",
    )(*operands)
    return out.reshape(nb * n_new, ATTN_WIDTH)


SB_TQ = 512
SB_TK = 256
SB_HEADS = 2
LOG2_E = 1.4426950408889634


def _suffix_matrix(n):
    r = lax.broadcasted_iota(jnp.int32, (2 * n, n), 0) % n
    c = lax.broadcasted_iota(jnp.int32, (2 * n, n), 1)
    return jnp.where(r > c, 1.0, 0.0).astype(BF16)


def _stick_logits(z, causal, tri):
    neg_relu = jnp.minimum(z, 0.0)
    neg_part = neg_relu - z
    soft = jnp.log2(1.0 + jnp.exp2(neg_relu + neg_part))
    log_sig = neg_relu - soft
    u = neg_part - soft
    if causal is not None:
        u = jnp.where(causal, u, 0.0)
    u_hi = u.astype(BF16)
    u_lo = (u - u_hi.astype(F32)).astype(BF16)
    after = jnp.dot(jnp.concatenate([u_hi, u_lo], axis=1), tri, preferred_element_type=F32)
    x = log_sig + after
    if causal is not None:
        x = jnp.where(causal, x, NEG_INF)
    return x, after[:, :1] + u[:, :1]


def _stick_tile(z, causal, carry, tri):
    x, total = _stick_logits(z, causal, tri)
    return jnp.exp2(x + carry), carry + total


def _sb_prompt_kernel(bias_ref, q_ref, k_ref, v_ref, o_ref, k_bf, v_bf):
    hp = pl.program_id(0)
    i = pl.program_id(1)

    @pl.when(i == 0)
    def _():
        k_bf[...] = k_ref[...].astype(BF16)
        v_bf[...] = v_ref[...].astype(BF16)

    cols = [slice(hh * HEAD_DIM, (hh + 1) * HEAD_DIM) for hh in range(SB_HEADS)]
    qs = [q_ref[:, c].astype(BF16) for c in cols]
    biases = [bias_ref[hp * SB_HEADS + hh] * LOG2_E for hh in range(SB_HEADS)]
    tri = _suffix_matrix(SB_TK)
    n_diag = SB_TQ // SB_TK
    n_tiles = (i + 1) * n_diag

    def tile(j, state, masked):
        rows = pl.ds(pl.multiple_of(j * SB_TK, SB_TK), SB_TK)
        causal = None
        if masked:
            q_pos = i * SB_TQ + lax.broadcasted_iota(jnp.int32, (SB_TQ, SB_TK), 0)
            k_pos = j * SB_TK + lax.broadcasted_iota(jnp.int32, (SB_TQ, SB_TK), 1)
            causal = k_pos < q_pos
        out = []
        for hh in range(SB_HEADS):
            carry, acc = state[hh]
            z = _nt_dot(qs[hh], k_bf[rows, cols[hh]]) * (ATTN_SCALE * LOG2_E) + biases[hh]
            a, carry = _stick_tile(z, causal, carry, tri)
            acc = acc + jnp.dot(a.astype(BF16), v_bf[rows, cols[hh]],
                                preferred_element_type=F32)
            out.append((carry, acc))
        return tuple(out)

    state = tuple((jnp.zeros((SB_TQ, 1), F32), jnp.zeros((SB_TQ, HEAD_DIM), F32))
                  for _ in range(SB_HEADS))
    state = lax.fori_loop(0, n_diag, lambda jj, s: tile(n_tiles - 1 - jj, s, True), state)
    state = lax.fori_loop(n_diag, n_tiles, lambda jj, s: tile(n_tiles - 1 - jj, s, False), state)
    for hh in range(SB_HEADS):
        o_ref[:, cols[hh]] = state[hh][1].astype(o_ref.dtype)


def sb_prompt_attention(p, bias, seq):
    width = SB_HEADS * HEAD_DIM
    n_blk = N_HEADS // SB_HEADS
    return pl.pallas_call(
        _sb_prompt_kernel,
        grid=(n_blk, seq // SB_TQ),
        in_specs=[pl.BlockSpec(memory_space=pltpu.SMEM),
                  pl.BlockSpec((SB_TQ, width), lambda h, i: (i, h)),
                  pl.BlockSpec((seq, width), lambda h, i: (0, n_blk + h),
                               pipeline_mode=pl.Buffered(1)),
                  pl.BlockSpec((seq, width), lambda h, i: (0, 2 * n_blk + h),
                               pipeline_mode=pl.Buffered(1))],
        out_specs=pl.BlockSpec((SB_TQ, width), lambda h, i: (i, h)),
        out_shape=jax.ShapeDtypeStruct((seq, ATTN_WIDTH), BF16),
        scratch_shapes=[pltpu.VMEM((seq, width), BF16), pltpu.VMEM((seq, width), BF16)],
        compiler_params=_params(2),
        name="sb_prompt",
    )(bias, p, p, p)


SB_ROWS = 16


def _sb_sample_kernel(pt_ref, ps_ref, bias_ref, kc_ref, vc_ref, o_ref,
                      q_bf, acc_ref, carry_ref, qf_ref, kn_ref, vn_ref, *, n_new):
    del pt_ref
    p = pl.program_id(1)
    tri = _suffix_matrix(PAGE_SIZE)
    bias = bias_ref[...]

    def head_rows(ref, h, n):
        return ref[pl.ds(h, n, stride=N_HEADS), :].astype(BF16)

    def tile(k_ref, v_ref, causal):
        s = jnp.concatenate(
            [_nt_dot(q_bf[h], head_rows(k_ref, h, PAGE_SIZE)) for h in range(N_HEADS)], axis=0)
        z = s * (ATTN_SCALE * LOG2_E) + bias
        a, carry = _stick_tile(z, causal, carry_ref[...], tri)
        carry_ref[...] = carry
        a = a.astype(BF16)
        for h in range(N_HEADS):
            acc_ref[h] += jnp.dot(a[h * SB_ROWS:(h + 1) * SB_ROWS],
                                  head_rows(v_ref, h, PAGE_SIZE), preferred_element_type=F32)

    @pl.when(p == 0)
    def _():
        qf_ref[...] = jnp.zeros_like(qf_ref)
        kn_ref[...] = jnp.zeros_like(kn_ref)
        vn_ref[...] = jnp.zeros_like(vn_ref)
        for r in range(n_new):
            rows = slice(r * N_HEADS, (r + 1) * N_HEADS)
            qf_ref[rows, :] = ps_ref[0, r, 0:N_HEADS, :]
            kn_ref[rows, :] = ps_ref[0, r, N_HEADS:2 * N_HEADS, :]
            vn_ref[rows, :] = ps_ref[0, r, 2 * N_HEADS:3 * N_HEADS, :]
        for h in range(N_HEADS):
            q_bf[h] = head_rows(qf_ref, h, SB_ROWS)
        acc_ref[...] = jnp.zeros_like(acc_ref)
        carry_ref[...] = jnp.zeros_like(carry_ref)
        shape = (N_HEADS * SB_ROWS, PAGE_SIZE)
        step = lax.broadcasted_iota(jnp.int32, shape, 0) % SB_ROWS
        key = lax.broadcasted_iota(jnp.int32, shape, 1)
        tile(kn_ref, vn_ref, key < step)

    tile(kc_ref.at[0, 0], vc_ref.at[0, 0], None)

    @pl.when(p == pl.num_programs(1) - 1)
    def _():
        for r in range(n_new):
            for h in range(N_HEADS):
                o_ref[0, r, :, h * HEAD_DIM:(h + 1) * HEAD_DIM] = acc_ref[h, r:r + 1, :]


def sb_sample_attention(ps, cache_k, cache_v, layer, page_table, bias):
    nb, n_new = ps.shape[:2]
    n_pages = page_table.shape[1]
    n_rows = N_HEADS * SB_ROWS
    assert n_new <= SB_ROWS

    def page(b, p, pt):
        return (layer, pt[b * n_pages + (n_pages - 1 - p)], 0, 0)

    page_rows = PAGE_SIZE * N_HEADS
    page_block = (1, 1, page_rows, HEAD_DIM)
    caches = [c.reshape(c.shape[0], c.shape[1], page_rows, HEAD_DIM) for c in (cache_k, cache_v)]
    grid_spec = pltpu.PrefetchScalarGridSpec(
        num_scalar_prefetch=1,
        grid=(nb, n_pages),
        in_specs=[pl.BlockSpec((1, n_new) + ps.shape[2:], lambda b, p, pt: (b, 0, 0, 0)),
                  pl.BlockSpec((n_rows, 1), lambda b, p, pt: (0, 0)),
                  pl.BlockSpec(page_block, page),
                  pl.BlockSpec(page_block, page)],
        out_specs=pl.BlockSpec((1, n_new, 1, ATTN_WIDTH), lambda b, p, pt: (b, 0, 0, 0)),
        scratch_shapes=[pltpu.VMEM((N_HEADS, SB_ROWS, HEAD_DIM), BF16),
                        pltpu.VMEM((N_HEADS, SB_ROWS, HEAD_DIM), F32),
                        pltpu.VMEM((n_rows, 1), F32),
                        pltpu.VMEM((SB_ROWS * N_HEADS, HEAD_DIM), F32),
                        pltpu.VMEM((page_rows, HEAD_DIM), F32),
                        pltpu.VMEM((page_rows, HEAD_DIM), F32)],
    )
    bias_rows = jnp.repeat(bias.astype(F32) * LOG2_E, SB_ROWS).reshape(n_rows, 1)
    out = pl.pallas_call(
        functools.partial(_sb_sample_kernel, n_new=n_new),
        grid_spec=grid_spec,
        out_shape=jax.ShapeDtypeStruct((nb, n_new, 1, ATTN_WIDTH), F32),
        compiler_params=_params(2),
        name="sb_sample",
    )(page_table.reshape(-1), ps, bias_rows, *caches)
    return out.reshape(nb * n_new, ATTN_WIDTH)


def _rope_tables(pos):
    half = HEAD_DIM // 2
    inv = ROPE_THETA ** (-jnp.arange(half, dtype=F32) / half)
    ang = pos.astype(F32)[:, None] * inv[None, :]
    cos = jnp.cos(ang)
    sin = jnp.sin(ang)
    return jnp.concatenate([cos, cos], axis=-1), jnp.concatenate([-sin, sin], axis=-1)


def _macaron_half(x, g, w_gu, w_d, layer):
    h = rmsnorm(x, g, BF16, tm=640)
    a = gateup(h, w_gu, layer, tm=832, tn=512)
    return mm_resid(a, w_d, layer, x, 0.5, tm=832, tn=256)


def kernel(x_prompt, x_sample, state_win128_kv, state_win512_kv, state_win2048_kv, cache_sb_k, cache_sb_v, page_table, norm_pre, ffn_pre_wgu, ffn_pre_wd, norm_mix, a_w_in, a_w_out, b_w_qkv, b_logit_bias, b_w_out, norm_post, ffn_post_wgu, ffn_post_wd, final_norm):
    _, seq, d_model = x_prompt.shape
    nb, n_new, _ = x_sample.shape
    depth = norm_pre.shape[0]
    past_len = page_table.shape[1] * PAGE_SIZE
    win_bufs = (state_win128_kv, state_win512_kv, state_win2048_kv)

    y = jnp.concatenate([x_prompt.reshape(seq, d_model), x_sample.reshape(nb * n_new, d_model)])
    pos = jnp.concatenate([jnp.arange(seq), past_len + jnp.tile(jnp.arange(n_new), nb)])
    cos2, sin2 = _rope_tables(pos)

    win_p = [[] for _ in DILATED_GROUPS]
    win_s = [[] for _ in DILATED_GROUPS]
    sbk_p, sbv_p, sbk_s, sbv_s = [], [], [], []
    for i in range(depth):
        j = i // N_MIXERS
        y = _macaron_half(y, norm_pre[i], ffn_pre_wgu, ffn_pre_wd, i)
        h = rmsnorm(y, norm_mix[i], BF16, tm=640)
        if i % N_MIXERS == 0:
            p = proj(h, a_w_in, j, cos2, sin2, True, tm=1664, tn=512)
            n_slots = p.shape[1] // HEAD_DIM
            ps = split_heads(p, seq, nb * n_new, (0,), n_slots)
            bufs = tuple(buf[j] for buf in win_bufs)
            for (window, _), buf in zip(DILATED_GROUPS, bufs):
                assert buf.shape[1] == window and seq >= window
            o_s = window_attention(ps.reshape(nb, n_new, n_slots, HEAD_DIM), bufs)
            o = jnp.concatenate([band_attention(p, seq), o_s.astype(BF16)])
            y = mm_resid(o, a_w_out, j, y, 1.0, tm=1664, tn=512)
            ps6 = ps.reshape(nb, n_new, N_GROUPS, 3, N_HEADS, HEAD_DIM)
            for g, (window, _) in enumerate(DILATED_GROUPS):
                cols = tuple((g * 3 + kind) * ATTN_WIDTH for kind in (1, 2))
                win_p[g].append(split_heads(p, seq - window, window, cols, N_HEADS)[None])
                win_s[g].append(shift_state(ps6, g, bufs[g]))
        else:
            p = proj(h, b_w_qkv, j, cos2, sin2, False, tm=1664, tn=512)
            ps = split_heads(p, seq, nb * n_new, (0,), 3 * N_HEADS).reshape(
                nb, n_new, 3 * N_HEADS, HEAD_DIM)
            o_s = sb_sample_attention(ps, cache_sb_k, cache_sb_v, j, page_table, b_logit_bias[j])
            o = jnp.concatenate([sb_prompt_attention(p, b_logit_bias[j].astype(F32), seq),
                                 o_s.astype(BF16)])
            y = mm_resid(o, b_w_out, j, y, 1.0, tm=1664, tn=512)
            sbk_p.append(split_heads(p, 0, seq, (ATTN_WIDTH,), N_HEADS).reshape(
                1, seq, N_HEADS, HEAD_DIM))
            sbv_p.append(split_heads(p, 0, seq, (2 * ATTN_WIDTH,), N_HEADS).reshape(
                1, seq, N_HEADS, HEAD_DIM))
            sbk_s.append(ps[:, :, N_HEADS:2 * N_HEADS])
            sbv_s.append(ps[:, :, 2 * N_HEADS:])
        y = _macaron_half(y, norm_post[i], ffn_post_wgu, ffn_post_wd, i)
    y = rmsnorm(y, final_norm, F32, tm=640)
    return (y[:seq].reshape(1, seq, d_model), y[seq:].reshape(nb, n_new, d_model),
            jnp.stack(win_p[0]), jnp.stack(win_s[0]),
            jnp.stack(win_p[1]), jnp.stack(win_s[1]),
            jnp.stack(win_p[2]), jnp.stack(win_s[2]),
            jnp.stack(sbk_p), jnp.stack(sbv_p), jnp.stack(sbk_s), jnp.stack(sbv_s))
```

```python
import functools

import jax
import jax.numpy as jnp
from jax import lax
from jax.experimental import pallas as pl
from jax.experimental.pallas import tpu as pltpu

F32 = jnp.float32
BF16 = jnp.bfloat16

N_HEADS = 16
HEAD_DIM = 128
ATTN_WIDTH = N_HEADS * HEAD_DIM
ATTN_SCALE = HEAD_DIM ** -0.5
DILATED_GROUPS = ((128, 1), (512, 4), (2048, 16))
N_GROUPS = len(DILATED_GROUPS)
N_MIXERS = 2
BLOCK = 128
PAGE_SIZE = 128
ROPE_THETA = 10000.0
RMS_EPS = 1e-6

V7X_VMEM_LIMIT_BYTES = 56 * 1024 * 1024
NEG_INF = float("-inf")


def _params(n_axes):
    return pltpu.CompilerParams(
        dimension_semantics=("arbitrary",) * n_axes,
        vmem_limit_bytes=V7X_VMEM_LIMIT_BYTES,
    )


def _nt_dot(a, b):
    return lax.dot_general(a, b, (((1,), (1,)), ((), ())), preferred_element_type=F32)


def _rmsnorm_kernel(x_ref, g_ref, o_ref):
    x = x_ref[...]
    ms = jnp.mean(x * x, axis=-1, keepdims=True)
    o_ref[...] = (x * lax.rsqrt(ms + RMS_EPS) * g_ref[...]).astype(o_ref.dtype)


def rmsnorm(x, g, out_dtype, tm):
    m, d = x.shape
    return pl.pallas_call(
        _rmsnorm_kernel,
        grid=(m // tm,),
        in_specs=[pl.BlockSpec((tm, d), lambda i: (i, 0)),
                  pl.BlockSpec((1, d), lambda i: (0, 0))],
        out_specs=pl.BlockSpec((tm, d), lambda i: (i, 0)),
        out_shape=jax.ShapeDtypeStruct((m, d), out_dtype),
        compiler_params=_params(1),
        name="rmsnorm",
    )(x, g.reshape(1, d))


def _gateup_kernel(h_ref, wg_ref, wu_ref, o_ref, wg_bf, wu_bf):
    @pl.when(pl.program_id(1) == 0)
    def _():
        wg_bf[...] = wg_ref[...].astype(BF16)
        wu_bf[...] = wu_ref[...].astype(BF16)

    h = h_ref[...]
    g = jnp.dot(h, wg_bf[...], preferred_element_type=F32)
    u = jnp.dot(h, wu_bf[...], preferred_element_type=F32)
    o_ref[...] = (g * jax.nn.sigmoid(g) * u).astype(o_ref.dtype)


def gateup(h, w_gu, layer, tm, tn):
    m, k = h.shape
    f = w_gu.shape[2] // 2
    nf = f // tn
    return pl.pallas_call(
        _gateup_kernel,
        grid=(nf, m // tm),
        in_specs=[pl.BlockSpec((tm, k), lambda j, i: (i, 0)),
                  pl.BlockSpec((None, k, tn), lambda j, i: (layer, 0, j)),
                  pl.BlockSpec((None, k, tn), lambda j, i: (layer, 0, j + nf))],
        out_specs=pl.BlockSpec((tm, tn), lambda j, i: (i, j)),
        out_shape=jax.ShapeDtypeStruct((m, f), BF16),
        scratch_shapes=[pltpu.VMEM((k, tn), BF16), pltpu.VMEM((k, tn), BF16)],
        compiler_params=_params(2),
        name="ffn_gateup",
    )(h, w_gu, w_gu)


def _mm_resid_kernel(a_ref, w_ref, r_ref, o_ref, w_bf, *, scale):
    @pl.when(pl.program_id(1) == 0)
    def _():
        w_bf[...] = w_ref[...].astype(BF16)

    acc = jnp.dot(a_ref[...], w_bf[...], preferred_element_type=F32)
    o_ref[...] = r_ref[...] + scale * acc


def mm_resid(a, w, layer, r, scale, tm, tn):
    m, k = a.shape
    n = w.shape[2]
    return pl.pallas_call(
        functools.partial(_mm_resid_kernel, scale=scale),
        grid=(n // tn, m // tm),
        in_specs=[pl.BlockSpec((tm, k), lambda j, i: (i, 0)),
                  pl.BlockSpec((None, k, tn), lambda j, i: (layer, 0, j)),
                  pl.BlockSpec((tm, tn), lambda j, i: (i, j))],
        out_specs=pl.BlockSpec((tm, tn), lambda j, i: (i, j)),
        out_shape=jax.ShapeDtypeStruct((m, n), F32),
        scratch_shapes=[pltpu.VMEM((k, tn), BF16)],
        compiler_params=_params(2),
        name="mm_resid",
    )(a, w, r)


def _proj_kernel(h_ref, w_ref, cos_ref, sin_ref, o_ref, w_bf, *, rope, tn):
    @pl.when(pl.program_id(1) == 0)
    def _():
        w_bf[...] = w_ref[...].astype(BF16)

    acc = jnp.dot(h_ref[...], w_bf[...], preferred_element_type=F32)
    if not rope:
        o_ref[...] = acc
        return

    kind = (pl.program_id(0) * tn // ATTN_WIDTH) % 3

    @pl.when(kind < 2)
    def _():
        c = cos_ref[...]
        s = sin_ref[...]
        for hh in range(tn // HEAD_DIM):
            x = acc[:, hh * HEAD_DIM:(hh + 1) * HEAD_DIM]
            o_ref[:, hh * HEAD_DIM:(hh + 1) * HEAD_DIM] = (
                x * c + pltpu.roll(x, HEAD_DIM // 2, 1) * s)

    @pl.when(kind == 2)
    def _():
        o_ref[...] = acc


def proj(h, w, layer, cos2, sin2, rope, tm, tn):
    m, k = h.shape
    n = w.shape[2]
    return pl.pallas_call(
        functools.partial(_proj_kernel, rope=rope, tn=tn),
        grid=(n // tn, m // tm),
        in_specs=[pl.BlockSpec((tm, k), lambda j, i: (i, 0)),
                  pl.BlockSpec((None, k, tn), lambda j, i: (layer, 0, j)),
                  pl.BlockSpec((tm, HEAD_DIM), lambda j, i: (i, 0)),
                  pl.BlockSpec((tm, HEAD_DIM), lambda j, i: (i, 0))],
        out_specs=pl.BlockSpec((tm, tn), lambda j, i: (i, j)),
        out_shape=jax.ShapeDtypeStruct((m, n), F32),
        scratch_shapes=[pltpu.VMEM((k, tn), BF16)],
        compiler_params=_params(2),
        name="proj_rope" if rope else "proj",
    )(h, w, cos2, sin2)


BAND_CHUNK = BLOCK * max(d for _, d in DILATED_GROUPS)


def _band_block(q, kc, vc, kp, vp, prev_ok, mask_cur, mask_prev):
    sc = jnp.where(mask_cur, _nt_dot(q, kc) * ATTN_SCALE, NEG_INF)
    mp = mask_prev if prev_ok is None else jnp.logical_and(mask_prev, prev_ok)
    sp = jnp.where(mp, _nt_dot(q, kp) * ATTN_SCALE, NEG_INF)
    m = jnp.maximum(jnp.max(sc, axis=-1, keepdims=True), jnp.max(sp, axis=-1, keepdims=True))
    pc = jnp.exp(sc - m)
    pp = jnp.exp(sp - m)
    l = jnp.sum(pc, axis=-1, keepdims=True) + jnp.sum(pp, axis=-1, keepdims=True)
    acc = (jnp.dot(pc.astype(BF16), vc, preferred_element_type=F32)
           + jnp.dot(pp.astype(BF16), vp, preferred_element_type=F32))
    return acc / l, m + jnp.log(l)


def _band_kernel(*refs):
    ins = refs[:5 * N_GROUPS]
    o_ref = refs[5 * N_GROUPS]
    og = refs[5 * N_GROUPS + 1: 5 * N_GROUPS + 1 + N_GROUPS]
    lg = refs[5 * N_GROUPS + 1 + N_GROUPS:]
    first_chunk = pl.program_id(1) == 0
    row = lax.broadcasted_iota(jnp.int32, (BLOCK, BLOCK), 0)
    col = lax.broadcasted_iota(jnp.int32, (BLOCK, BLOCK), 1)
    mask_cur = col <= row
    mask_prev = col >= row

    for g, (_, dil) in enumerate(DILATED_GROUPS):
        q_ref, kc_ref, vc_ref, kp_ref, vp_ref = ins[5 * g: 5 * g + 5]
        span = BLOCK * dil
        for s in range(BAND_CHUNK // span):
            for c in range(dil):
                def rows(start):
                    if dil == 1:
                        return pl.ds(start, BLOCK)
                    return pl.ds(start, BLOCK, stride=dil)
                cur = rows(s * span + c)
                q = q_ref[cur, :].astype(BF16)
                kc = kc_ref[cur, :].astype(BF16)
                vc = vc_ref[cur, :].astype(BF16)
                if s > 0:
                    prev = rows((s - 1) * span + c)
                    kp = kc_ref[prev, :].astype(BF16)
                    vp = vc_ref[prev, :].astype(BF16)
                    prev_ok = None
                else:
                    prev = rows(c)
                    kp = kp_ref[prev, :].astype(BF16)
                    vp = vp_ref[prev, :].astype(BF16)
                    prev_ok = jnp.logical_not(first_chunk)
                o, lse = _band_block(q, kc, vc, kp, vp, prev_ok, mask_cur, mask_prev)
                og[g][cur, :] = o
                lg[g][cur, :] = jnp.broadcast_to(lse, (BLOCK, HEAD_DIM))

    rows_per = 64

    def merge(r, carry):
        sl = pl.ds(pl.multiple_of(r * rows_per, rows_per), rows_per)
        ls = [lg[g][sl, :] for g in range(N_GROUPS)]
        mx = functools.reduce(jnp.maximum, ls)
        es = [jnp.exp(l - mx) for l in ls]
        den = functools.reduce(lambda a, b: a + b, es)
        num = functools.reduce(lambda a, b: a + b,
                               [es[g] * og[g][sl, :] for g in range(N_GROUPS)])
        o_ref[sl, :] = (num / den).astype(o_ref.dtype)
        return carry

    lax.fori_loop(0, BAND_CHUNK // rows_per, merge, 0)


def band_attention(p, seq):
    def colblk(g, kind, h):
        return (g * 3 + kind) * N_HEADS + h

    in_specs = []
    operands = []
    for g, (_, dil) in enumerate(DILATED_GROUPS):
        span = BLOCK * dil
        per = BAND_CHUNK // span
        for kind in range(3):
            in_specs.append(pl.BlockSpec(
                (BAND_CHUNK, HEAD_DIM), lambda h, n, g=g, kind=kind: (n, colblk(g, kind, h))))
            operands.append(p)
        for kind in (1, 2):
            in_specs.append(pl.BlockSpec(
                (span, HEAD_DIM),
                lambda h, n, g=g, kind=kind, per=per: (jnp.maximum(n * per - 1, 0),
                                                       colblk(g, kind, h))))
            operands.append(p)
    return pl.pallas_call(
        _band_kernel,
        grid=(N_HEADS, seq // BAND_CHUNK),
        in_specs=in_specs,
        out_specs=pl.BlockSpec((BAND_CHUNK, HEAD_DIM), lambda h, n: (n, h)),
        out_shape=jax.ShapeDtypeStruct((seq, ATTN_WIDTH), BF16),
        scratch_shapes=[pltpu.VMEM((BAND_CHUNK, HEAD_DIM), F32)] * (2 * N_GROUPS),
        compiler_params=_params(2),
        name="band_attention",
    )(*operands)


def _split_heads_kernel(*refs):
    o_ref = refs[-1]
    for part, x_ref in enumerate(refs[:-1]):
        for h in range(o_ref.shape[2]):
            o_ref[:, part, h, :] = x_ref[:, h * HEAD_DIM:(h + 1) * HEAD_DIM]


def split_heads(p, row0, n_rows, col0s, n_heads):
    width = n_heads * HEAD_DIM
    tr = min(n_rows, 512)
    assert row0 % tr == 0 and n_rows % tr == 0 and all(c % width == 0 for c in col0s)
    return pl.pallas_call(
        _split_heads_kernel,
        grid=(n_rows // tr,),
        in_specs=[pl.BlockSpec((tr, width), lambda i, c=c: (row0 // tr + i, c // width))
                  for c in col0s],
        out_specs=pl.BlockSpec((tr, len(col0s), n_heads, HEAD_DIM), lambda i: (i, 0, 0, 0)),
        out_shape=jax.ShapeDtypeStruct((n_rows, len(col0s), n_heads, HEAD_DIM), F32),
        compiler_params=_params(1),
        name="split_heads",
    )(*([p] * len(col0s)))


SHIFT_CHUNK = 128


def _shift_state_kernel(ps_ref, cur_ref, nxt_ref, o_ref, *, n_new):
    chunk = cur_ref.shape[1]
    last = pl.program_id(1) == pl.num_programs(1) - 1
    o_ref[0, 0:chunk - n_new] = cur_ref[0, n_new:chunk]

    @pl.when(last)
    def _():
        o_ref[0, chunk - n_new:chunk] = ps_ref[0, :, 0, 1:3]

    @pl.when(jnp.logical_not(last))
    def _():
        o_ref[0, chunk - n_new:chunk] = nxt_ref[0]


def shift_state(ps, group, buf):
    nb, n_new = ps.shape[:2]
    window = buf.shape[1]
    chunk = min(SHIFT_CHUNK, window)
    n_chunks = window // chunk
    assert window % chunk == 0 and chunk % n_new == 0
    per = chunk // n_new
    tail = buf.shape[2:]
    zeros = (0,) * len(tail)
    return pl.pallas_call(
        functools.partial(_shift_state_kernel, n_new=n_new),
        grid=(nb, n_chunks),
        in_specs=[pl.BlockSpec((1, n_new, 1) + ps.shape[3:], lambda b, c: (b, 0, group, 0, 0, 0)),
                  pl.BlockSpec((1, chunk) + tail, lambda b, c: (b, c) + zeros),
                  pl.BlockSpec((1, n_new) + tail,
                               lambda b, c: (b, jnp.minimum(c + 1, n_chunks - 1) * per) + zeros)],
        out_specs=pl.BlockSpec((1, chunk) + tail, lambda b, c: (b, c) + zeros),
        out_shape=jax.ShapeDtypeStruct(buf.shape, buf.dtype),
        compiler_params=_params(2),
        name="shift_state",
    )(ps, buf, buf)


def _window_kernel(ps_ref, b0_ref, b1_ref, b2_ref, o_ref, *, n_new):
    t = pl.program_id(1)
    key_i = lax.broadcasted_iota(jnp.int32, (BLOCK, N_HEADS, 1), 0)
    outs, lses = [], []
    for g, ((_, dil), buf_ref) in enumerate(zip(DILATED_GROUPS, (b0_ref, b1_ref, b2_ref))):
        h0 = g * 3 * N_HEADS
        q = ps_ref[0, t, h0: h0 + N_HEADS, :]
        k = buf_ref[0, :, 0, 0]
        v = buf_ref[0, :, 0, 1]
        s_past = jnp.sum(k * q[None], axis=-1, keepdims=True) * ATTN_SCALE
        if dil == 1:
            s_past = jnp.where(key_i >= t, s_past, NEG_INF)
        m = jnp.max(s_past, axis=0)
        s_new = []
        for r in range(n_new):
            kn = ps_ref[0, r, h0 + N_HEADS: h0 + 2 * N_HEADS, :]
            sr = jnp.sum(q * kn, axis=-1, keepdims=True) * ATTN_SCALE
            ok = (r <= t) if dil == 1 else (r == t)
            sr = jnp.where(ok, sr, NEG_INF)
            s_new.append(sr)
            m = jnp.maximum(m, sr)
        p_past = jnp.exp(s_past - m[None])
        l = jnp.sum(p_past, axis=0)
        o = jnp.sum(p_past * v, axis=0)
        for r in range(n_new):
            pr = jnp.exp(s_new[r] - m)
            l = l + pr
            o = o + pr * ps_ref[0, r, h0 + 2 * N_HEADS: h0 + 3 * N_HEADS, :]
        outs.append(o / l)
        lses.append(m + jnp.log(l))
    mx = functools.reduce(jnp.maximum, lses)
    es = [jnp.exp(l - mx) for l in lses]
    den = functools.reduce(lambda a, b: a + b, es)
    num = functools.reduce(lambda a, b: a + b, [e * o for e, o in zip(es, outs)])
    merged = num / den
    for h in range(N_HEADS):
        o_ref[0, 0, :, h * HEAD_DIM:(h + 1) * HEAD_DIM] = merged[h:h + 1, :]


def window_attention(ps, bufs):
    nb, n_new = ps.shape[:2]
    in_specs = [pl.BlockSpec((1, n_new) + ps.shape[2:], lambda b, t: (b, 0, 0, 0))]
    operands = [ps]
    for (window, dil), buf in zip(DILATED_GROUPS, bufs):
        assert buf.shape[1] == window == BLOCK * dil
        operands.append(buf.reshape(nb, BLOCK, dil, 2, N_HEADS, HEAD_DIM))
        blk = (1, BLOCK, 1, 2, N_HEADS, HEAD_DIM)
        if dil == 1:
            in_specs.append(pl.BlockSpec(blk, lambda b, t: (b, 0, 0, 0, 0, 0)))
        else:
            in_specs.append(pl.BlockSpec(blk, lambda b, t: (b, 0, t, 0, 0, 0)))
    out = pl.pallas_call(
        functools.partial(_window_kernel, n_new=n_new),
        grid=(nb, n_new),
        in_specs=in_specs,
        out_specs=pl.BlockSpec((1, 1, 1, ATTN_WIDTH), lambda b, t: (b, t, 0, 0)),
        out_shape=jax.ShapeDtypeStruct((nb, n_new, 1, ATTN_WIDTH), F32),
        compiler_params=_params(2),
        name="window_attention",
    )(*operands)
    return out.reshape(nb * n_new, ATTN_WIDTH)


SB_TQ = 512
SB_TK = 256
SB_HEADS = 2
LOG2_E = 1.4426950408889634


def _suffix_matrix(n):
    r = lax.broadcasted_iota(jnp.int32, (2 * n, n), 0) % n
    c = lax.broadcasted_iota(jnp.int32, (2 * n, n), 1)
    return jnp.where(r > c, 1.0, 0.0).astype(BF16)


def _stick_logits(z, causal, tri):
    neg_relu = jnp.minimum(z, 0.0)
    neg_part = neg_relu - z
    soft = jnp.log2(1.0 + jnp.exp2(neg_relu + neg_part))
    log_sig = neg_relu - soft
    u = neg_part - soft
    if causal is not None:
        u = jnp.where(causal, u, 0.0)
    u_hi = u.astype(BF16)
    u_lo = (u - u_hi.astype(F32)).astype(BF16)
    after = jnp.dot(jnp.concatenate([u_hi, u_lo], axis=1), tri, preferred_element_type=F32)
    x = log_sig + after
    if causal is not None:
        x = jnp.where(causal, x, NEG_INF)
    return x, after[:, :1] + u[:, :1]


def _stick_tile(z, causal, carry, tri):
    x, total = _stick_logits(z, causal, tri)
    return jnp.exp2(x + carry), carry + total


def _sb_prompt_kernel(bias_ref, q_ref, k_ref, v_ref, o_ref, k_bf, v_bf):
    hp = pl.program_id(0)
    i = pl.program_id(1)

    @pl.when(i == 0)
    def _():
        k_bf[...] = k_ref[...].astype(BF16)
        v_bf[...] = v_ref[...].astype(BF16)

    cols = [slice(hh * HEAD_DIM, (hh + 1) * HEAD_DIM) for hh in range(SB_HEADS)]
    qs = [q_ref[:, c].astype(BF16) for c in cols]
    biases = [bias_ref[hp * SB_HEADS + hh] * LOG2_E for hh in range(SB_HEADS)]
    tri = _suffix_matrix(SB_TK)
    n_diag = SB_TQ // SB_TK
    n_tiles = (i + 1) * n_diag

    def tile(j, state, masked):
        rows = pl.ds(pl.multiple_of(j * SB_TK, SB_TK), SB_TK)
        causal = None
        if masked:
            q_pos = i * SB_TQ + lax.broadcasted_iota(jnp.int32, (SB_TQ, SB_TK), 0)
            k_pos = j * SB_TK + lax.broadcasted_iota(jnp.int32, (SB_TQ, SB_TK), 1)
            causal = k_pos < q_pos
        out = []
        for hh in range(SB_HEADS):
            carry, acc = state[hh]
            z = _nt_dot(qs[hh], k_bf[rows, cols[hh]]) * (ATTN_SCALE * LOG2_E) + biases[hh]
            a, carry = _stick_tile(z, causal, carry, tri)
            acc = acc + jnp.dot(a.astype(BF16), v_bf[rows, cols[hh]],
                                preferred_element_type=F32)
            out.append((carry, acc))
        return tuple(out)

    state = tuple((jnp.zeros((SB_TQ, 1), F32), jnp.zeros((SB_TQ, HEAD_DIM), F32))
                  for _ in range(SB_HEADS))
    state = lax.fori_loop(0, n_diag, lambda jj, s: tile(n_tiles - 1 - jj, s, True), state)
    state = lax.fori_loop(n_diag, n_tiles, lambda jj, s: tile(n_tiles - 1 - jj, s, False), state)
    for hh in range(SB_HEADS):
        o_ref[:, cols[hh]] = state[hh][1].astype(o_ref.dtype)


def sb_prompt_attention(p, bias, seq):
    width = SB_HEADS * HEAD_DIM
    n_blk = N_HEADS // SB_HEADS
    return pl.pallas_call(
        _sb_prompt_kernel,
        grid=(n_blk, seq // SB_TQ),
        in_specs=[pl.BlockSpec(memory_space=pltpu.SMEM),
                  pl.BlockSpec((SB_TQ, width), lambda h, i: (i, h)),
                  pl.BlockSpec((seq, width), lambda h, i: (0, n_blk + h),
                               pipeline_mode=pl.Buffered(1)),
                  pl.BlockSpec((seq, width), lambda h, i: (0, 2 * n_blk + h),
                               pipeline_mode=pl.Buffered(1))],
        out_specs=pl.BlockSpec((SB_TQ, width), lambda h, i: (i, h)),
        out_shape=jax.ShapeDtypeStruct((seq, ATTN_WIDTH), BF16),
        scratch_shapes=[pltpu.VMEM((seq, width), BF16), pltpu.VMEM((seq, width), BF16)],
        compiler_params=_params(2),
        name="sb_prompt",
    )(bias, p, p, p)


SB_ROWS = 16


SB_PAGES = 4


def _sb_sample_kernel(*refs, n_new):
    ps_ref, bias_ref = refs[1:3]
    kc_refs = refs[3:3 + SB_PAGES]
    vc_refs = refs[3 + SB_PAGES:3 + 2 * SB_PAGES]
    o_ref, q_bf, acc_ref, carry_ref, qf_ref, kn_ref, vn_ref = refs[3 + 2 * SB_PAGES:]
    p = pl.program_id(1)
    tri = _suffix_matrix(PAGE_SIZE)
    bias = bias_ref[...]

    def head_rows(ref, h, n):
        return ref[pl.ds(h, n, stride=N_HEADS), :].astype(BF16)

    def tiles(k_refs, v_refs, causal):
        parts = []
        for k_ref in k_refs:
            s = jnp.concatenate(
                [_nt_dot(q_bf[h], head_rows(k_ref, h, PAGE_SIZE)) for h in range(N_HEADS)],
                axis=0)
            parts.append(_stick_logits(s * (ATTN_SCALE * LOG2_E) + bias, causal, tri))
        carry = carry_ref[...]
        for (x, total), v_ref in zip(parts, v_refs):
            a = jnp.exp2(x + carry).astype(BF16)
            carry = carry + total
            for h in range(N_HEADS):
                acc_ref[h] += jnp.dot(a[h * SB_ROWS:(h + 1) * SB_ROWS],
                                      head_rows(v_ref, h, PAGE_SIZE),
                                      preferred_element_type=F32)
        carry_ref[...] = carry

    @pl.when(p == 0)
    def _():
        qf_ref[...] = jnp.zeros_like(qf_ref)
        kn_ref[...] = jnp.zeros_like(kn_ref)
        vn_ref[...] = jnp.zeros_like(vn_ref)
        for r in range(n_new):
            rows = slice(r * N_HEADS, (r + 1) * N_HEADS)
            qf_ref[rows, :] = ps_ref[0, r, 0:N_HEADS, :]
            kn_ref[rows, :] = ps_ref[0, r, N_HEADS:2 * N_HEADS, :]
            vn_ref[rows, :] = ps_ref[0, r, 2 * N_HEADS:3 * N_HEADS, :]
        for h in range(N_HEADS):
            q_bf[h] = head_rows(qf_ref, h, SB_ROWS)
        acc_ref[...] = jnp.zeros_like(acc_ref)
        carry_ref[...] = jnp.zeros_like(carry_ref)
        shape = (N_HEADS * SB_ROWS, PAGE_SIZE)
        step = lax.broadcasted_iota(jnp.int32, shape, 0) % SB_ROWS
        key = lax.broadcasted_iota(jnp.int32, shape, 1)
        tiles([kn_ref], [vn_ref], key < step)

    tiles([r.at[0, 0] for r in kc_refs], [r.at[0, 0] for r in vc_refs], None)

    @pl.when(p == pl.num_programs(1) - 1)
    def _():
        for r in range(n_new):
            for h in range(N_HEADS):
                o_ref[0, r, :, h * HEAD_DIM:(h + 1) * HEAD_DIM] = acc_ref[h, r:r + 1, :]


def sb_sample_attention(ps, cache_k, cache_v, layer, page_table, bias):
    nb, n_new = ps.shape[:2]
    n_pages = page_table.shape[1]
    n_rows = N_HEADS * SB_ROWS
    assert n_new <= SB_ROWS and n_pages % SB_PAGES == 0

    def page(b, p, pt, slot):
        return (layer, pt[b * n_pages + (n_pages - 1 - (p * SB_PAGES + slot))], 0, 0)

    page_rows = PAGE_SIZE * N_HEADS
    page_specs = [pl.BlockSpec((1, 1, page_rows, HEAD_DIM), functools.partial(page, slot=slot))
                  for slot in range(SB_PAGES)]
    caches = [c.reshape(c.shape[0], c.shape[1], page_rows, HEAD_DIM) for c in (cache_k, cache_v)]
    grid_spec = pltpu.PrefetchScalarGridSpec(
        num_scalar_prefetch=1,
        grid=(nb, n_pages // SB_PAGES),
        in_specs=[pl.BlockSpec((1, n_new) + ps.shape[2:], lambda b, p, pt: (b, 0, 0, 0)),
                  pl.BlockSpec((n_rows, 1), lambda b, p, pt: (0, 0))] + page_specs * 2,
        out_specs=pl.BlockSpec((1, n_new, 1, ATTN_WIDTH), lambda b, p, pt: (b, 0, 0, 0)),
        scratch_shapes=[pltpu.VMEM((N_HEADS, SB_ROWS, HEAD_DIM), BF16),
                        pltpu.VMEM((N_HEADS, SB_ROWS, HEAD_DIM), F32),
                        pltpu.VMEM((n_rows, 1), F32),
                        pltpu.VMEM((SB_ROWS * N_HEADS, HEAD_DIM), F32),
                        pltpu.VMEM((page_rows, HEAD_DIM), F32),
                        pltpu.VMEM((page_rows, HEAD_DIM), F32)],
    )
    bias_rows = jnp.repeat(bias.astype(F32) * LOG2_E, SB_ROWS).reshape(n_rows, 1)
    out = pl.pallas_call(
        functools.partial(_sb_sample_kernel, n_new=n_new),
        grid_spec=grid_spec,
        out_shape=jax.ShapeDtypeStruct((nb, n_new, 1, ATTN_WIDTH), F32),
        compiler_params=_params(2),
        name="sb_sample",
    )(page_table.reshape(-1), ps, bias_rows,
      *([caches[0]] * SB_PAGES), *([caches[1]] * SB_PAGES))
    return out.reshape(nb * n_new, ATTN_WIDTH)


def _rope_tables(pos):
    half = HEAD_DIM // 2
    inv = ROPE_THETA ** (-jnp.arange(half, dtype=F32) / half)
    ang = pos.astype(F32)[:, None] * inv[None, :]
    cos = jnp.cos(ang)
    sin = jnp.sin(ang)
    return jnp.concatenate([cos, cos], axis=-1), jnp.concatenate([-sin, sin], axis=-1)


def _macaron_half(x, g, w_gu, w_d, layer):
    h = rmsnorm(x, g, BF16, tm=640)
    a = gateup(h, w_gu, layer, tm=832, tn=512)
    return mm_resid(a, w_d, layer, x, 0.5, tm=416, tn=512)


def kernel(x_prompt, x_sample, state_win128_kv, state_win512_kv, state_win2048_kv, cache_sb_k, cache_sb_v, page_table, norm_pre, ffn_pre_wgu, ffn_pre_wd, norm_mix, a_w_in, a_w_out, b_w_qkv, b_logit_bias, b_w_out, norm_post, ffn_post_wgu, ffn_post_wd, final_norm):
    _, seq, d_model = x_prompt.shape
    nb, n_new, _ = x_sample.shape
    depth = norm_pre.shape[0]
    past_len = page_table.shape[1] * PAGE_SIZE
    win_bufs = (state_win128_kv, state_win512_kv, state_win2048_kv)

    y = jnp.concatenate([x_prompt.reshape(seq, d_model), x_sample.reshape(nb * n_new, d_model)])
    pos = jnp.concatenate([jnp.arange(seq), past_len + jnp.tile(jnp.arange(n_new), nb)])
    cos2, sin2 = _rope_tables(pos)

    win_p = [[] for _ in DILATED_GROUPS]
    win_s = [[] for _ in DILATED_GROUPS]
    sbk_p, sbv_p, sbk_s, sbv_s = [], [], [], []
    for i in range(depth):
        j = i // N_MIXERS
        y = _macaron_half(y, norm_pre[i], ffn_pre_wgu, ffn_pre_wd, i)
        h = rmsnorm(y, norm_mix[i], BF16, tm=640)
        if i % N_MIXERS == 0:
            p = proj(h, a_w_in, j, cos2, sin2, True, tm=832, tn=1024)
            n_slots = p.shape[1] // HEAD_DIM
            ps = split_heads(p, seq, nb * n_new, (0,), n_slots)
            bufs = tuple(buf[j] for buf in win_bufs)
            for (window, _), buf in zip(DILATED_GROUPS, bufs):
                assert buf.shape[1] == window and seq >= window
            o_s = window_attention(ps.reshape(nb, n_new, n_slots, HEAD_DIM), bufs)
            o = jnp.concatenate([band_attention(p, seq), o_s.astype(BF16)])
            y = mm_resid(o, a_w_out, j, y, 1.0, tm=832, tn=1024)
            ps6 = ps.reshape(nb, n_new, N_GROUPS, 3, N_HEADS, HEAD_DIM)
            for g, (window, _) in enumerate(DILATED_GROUPS):
                cols = tuple((g * 3 + kind) * ATTN_WIDTH for kind in (1, 2))
                win_p[g].append(split_heads(p, seq - window, window, cols, N_HEADS)[None])
                win_s[g].append(shift_state(ps6, g, bufs[g]))
        else:
            p = proj(h, b_w_qkv, j, cos2, sin2, False, tm=832, tn=1024)
            ps = split_heads(p, seq, nb * n_new, (0,), 3 * N_HEADS).reshape(
                nb, n_new, 3 * N_HEADS, HEAD_DIM)
            o_s = sb_sample_attention(ps, cache_sb_k, cache_sb_v, j, page_table, b_logit_bias[j])
            o = jnp.concatenate([sb_prompt_attention(p, b_logit_bias[j].astype(F32), seq),
                                 o_s.astype(BF16)])
            y = mm_resid(o, b_w_out, j, y, 1.0, tm=832, tn=1024)
            sbk_p.append(split_heads(p, 0, seq, (ATTN_WIDTH,), N_HEADS).reshape(
                1, seq, N_HEADS, HEAD_DIM))
            sbv_p.append(split_heads(p, 0, seq, (2 * ATTN_WIDTH,), N_HEADS).reshape(
                1, seq, N_HEADS, HEAD_DIM))
            sbk_s.append(ps[:, :, N_HEADS:2 * N_HEADS])
            sbv_s.append(ps[:, :, 2 * N_HEADS:])
        y = _macaron_half(y, norm_post[i], ffn_post_wgu, ffn_post_wd, i)
    y = rmsnorm(y, final_norm, F32, tm=640)
    return (y[:seq].reshape(1, seq, d_model), y[seq:].reshape(nb, n_new, d_model),
            jnp.stack(win_p[0]), jnp.stack(win_s[0]),
            jnp.stack(win_p[1]), jnp.stack(win_s[1]),
            jnp.stack(win_p[2]), jnp.stack(win_s[2]),
            jnp.stack(sbk_p), jnp.stack(sbv_p), jnp.stack(sbk_s), jnp.stack(sbv_s))
```

```python
import functools

import jax
import jax.numpy as jnp
from jax import lax
from jax.experimental import pallas as pl
from jax.experimental.pallas import tpu as pltpu

F32 = jnp.float32
BF16 = jnp.bfloat16

N_HEADS = 16
HEAD_DIM = 128
ATTN_WIDTH = N_HEADS * HEAD_DIM
ATTN_SCALE = HEAD_DIM ** -0.5
DILATED_GROUPS = ((128, 1), (512, 4), (2048, 16))
N_GROUPS = len(DILATED_GROUPS)
N_MIXERS = 2
BLOCK = 128
PAGE_SIZE = 128
ROPE_THETA = 10000.0
RMS_EPS = 1e-6

V7X_VMEM_LIMIT_BYTES = 56 * 1024 * 1024
NEG_INF = float("-inf")


def _params(n_axes):
    return pltpu.CompilerParams(
        dimension_semantics=("arbitrary",) * n_axes,
        vmem_limit_bytes=V7X_VMEM_LIMIT_BYTES,
    )


def _nt_dot(a, b):
    return lax.dot_general(a, b, (((1,), (1,)), ((), ())), preferred_element_type=F32)


def _rmsnorm_kernel(x_ref, g_ref, o_ref):
    x = x_ref[...]
    ms = jnp.mean(x * x, axis=-1, keepdims=True)
    o_ref[...] = (x * lax.rsqrt(ms + RMS_EPS) * g_ref[...]).astype(o_ref.dtype)


def rmsnorm(x, g, out_dtype, tm):
    m, d = x.shape
    return pl.pallas_call(
        _rmsnorm_kernel,
        grid=(m // tm,),
        in_specs=[pl.BlockSpec((tm, d), lambda i: (i, 0)),
                  pl.BlockSpec((1, d), lambda i: (0, 0))],
        out_specs=pl.BlockSpec((tm, d), lambda i: (i, 0)),
        out_shape=jax.ShapeDtypeStruct((m, d), out_dtype),
        compiler_params=_params(1),
        name="rmsnorm",
    )(x, g.reshape(1, d))


def _gateup_kernel(h_ref, wg_ref, wu_ref, o_ref, wg_bf, wu_bf):
    @pl.when(pl.program_id(1) == 0)
    def _():
        wg_bf[...] = wg_ref[...].astype(BF16)
        wu_bf[...] = wu_ref[...].astype(BF16)

    h = h_ref[...]
    g = jnp.dot(h, wg_bf[...], preferred_element_type=F32)
    u = jnp.dot(h, wu_bf[...], preferred_element_type=F32)
    o_ref[...] = (g * jax.nn.sigmoid(g) * u).astype(o_ref.dtype)


def gateup(h, w_gu, layer, tm, tn):
    m, k = h.shape
    f = w_gu.shape[2] // 2
    nf = f // tn
    return pl.pallas_call(
        _gateup_kernel,
        grid=(nf, m // tm),
        in_specs=[pl.BlockSpec((tm, k), lambda j, i: (i, 0)),
                  pl.BlockSpec((None, k, tn), lambda j, i: (layer, 0, j)),
                  pl.BlockSpec((None, k, tn), lambda j, i: (layer, 0, j + nf))],
        out_specs=pl.BlockSpec((tm, tn), lambda j, i: (i, j)),
        out_shape=jax.ShapeDtypeStruct((m, f), BF16),
        scratch_shapes=[pltpu.VMEM((k, tn), BF16), pltpu.VMEM((k, tn), BF16)],
        compiler_params=_params(2),
        name="ffn_gateup",
    )(h, w_gu, w_gu)


def _mm_resid_kernel(a_ref, w_ref, r_ref, o_ref, w_bf, *, scale):
    @pl.when(pl.program_id(1) == 0)
    def _():
        w_bf[...] = w_ref[...].astype(BF16)

    acc = jnp.dot(a_ref[...], w_bf[...], preferred_element_type=F32)
    o_ref[...] = r_ref[...] + scale * acc


def mm_resid(a, w, layer, r, scale, tm, tn):
    m, k = a.shape
    n = w.shape[2]
    return pl.pallas_call(
        functools.partial(_mm_resid_kernel, scale=scale),
        grid=(n // tn, m // tm),
        in_specs=[pl.BlockSpec((tm, k), lambda j, i: (i, 0)),
                  pl.BlockSpec((None, k, tn), lambda j, i: (layer, 0, j)),
                  pl.BlockSpec((tm, tn), lambda j, i: (i, j))],
        out_specs=pl.BlockSpec((tm, tn), lambda j, i: (i, j)),
        out_shape=jax.ShapeDtypeStruct((m, n), F32),
        scratch_shapes=[pltpu.VMEM((k, tn), BF16)],
        compiler_params=_params(2),
        name="mm_resid",
    )(a, w, r)


def _proj_kernel(h_ref, w_ref, cos_ref, sin_ref, o_ref, w_bf, *, rope, tn):
    @pl.when(pl.program_id(1) == 0)
    def _():
        w_bf[...] = w_ref[...].astype(BF16)

    acc = jnp.dot(h_ref[...], w_bf[...], preferred_element_type=F32)
    if not rope:
        o_ref[...] = acc
        return

    kind = (pl.program_id(0) * tn // ATTN_WIDTH) % 3

    @pl.when(kind < 2)
    def _():
        c = cos_ref[...]
        s = sin_ref[...]
        for hh in range(tn // HEAD_DIM):
            x = acc[:, hh * HEAD_DIM:(hh + 1) * HEAD_DIM]
            o_ref[:, hh * HEAD_DIM:(hh + 1) * HEAD_DIM] = (
                x * c + pltpu.roll(x, HEAD_DIM // 2, 1) * s)

    @pl.when(kind == 2)
    def _():
        o_ref[...] = acc


def proj(h, w, layer, cos2, sin2, rope, tm, tn):
    m, k = h.shape
    n = w.shape[2]
    return pl.pallas_call(
        functools.partial(_proj_kernel, rope=rope, tn=tn),
        grid=(n // tn, m // tm),
        in_specs=[pl.BlockSpec((tm, k), lambda j, i: (i, 0)),
                  pl.BlockSpec((None, k, tn), lambda j, i: (layer, 0, j)),
                  pl.BlockSpec((tm, HEAD_DIM), lambda j, i: (i, 0)),
                  pl.BlockSpec((tm, HEAD_DIM), lambda j, i: (i, 0))],
        out_specs=pl.BlockSpec((tm, tn), lambda j, i: (i, j)),
        out_shape=jax.ShapeDtypeStruct((m, n), F32),
        scratch_shapes=[pltpu.VMEM((k, tn), BF16)],
        compiler_params=_params(2),
        name="proj_rope" if rope else "proj",
    )(h, w, cos2, sin2)


BAND_CHUNK = BLOCK * max(d for _, d in DILATED_GROUPS)


def _band_block(q, kc, vc, kp, vp, prev_ok, mask_cur, mask_prev):
    mp = mask_prev if prev_ok is None else jnp.logical_and(mask_prev, prev_ok)
    mask = jnp.concatenate([mp, mask_cur], axis=1)
    s = _nt_dot(q, jnp.concatenate([kp, kc], axis=0)) * ATTN_SCALE
    s = jnp.where(mask, s, NEG_INF)
    m = jnp.max(s, axis=-1, keepdims=True)
    p = jnp.exp(s - m)
    l = jnp.sum(p, axis=-1, keepdims=True)
    acc = jnp.dot(p.astype(BF16), jnp.concatenate([vp, vc], axis=0),
                  preferred_element_type=F32)
    return acc / l, m + jnp.log(l)


def _band_kernel(*refs):
    ins = refs[:5 * N_GROUPS]
    o_ref = refs[5 * N_GROUPS]
    og = refs[5 * N_GROUPS + 1: 5 * N_GROUPS + 1 + N_GROUPS]
    lg = refs[5 * N_GROUPS + 1 + N_GROUPS:]
    first_chunk = pl.program_id(1) == 0
    row = lax.broadcasted_iota(jnp.int32, (BLOCK, BLOCK), 0)
    col = lax.broadcasted_iota(jnp.int32, (BLOCK, BLOCK), 1)
    mask_cur = col <= row
    mask_prev = col >= row

    for g, (_, dil) in enumerate(DILATED_GROUPS):
        q_ref, kc_ref, vc_ref, kp_ref, vp_ref = ins[5 * g: 5 * g + 5]
        span = BLOCK * dil
        for s in range(BAND_CHUNK // span):
            for c in range(dil):
                def rows(start):
                    if dil == 1:
                        return pl.ds(start, BLOCK)
                    return pl.ds(start, BLOCK, stride=dil)
                cur = rows(s * span + c)
                q = q_ref[cur, :].astype(BF16)
                kc = kc_ref[cur, :].astype(BF16)
                vc = vc_ref[cur, :].astype(BF16)
                if s > 0:
                    prev = rows((s - 1) * span + c)
                    kp = kc_ref[prev, :].astype(BF16)
                    vp = vc_ref[prev, :].astype(BF16)
                    prev_ok = None
                else:
                    prev = rows(c)
                    kp = kp_ref[prev, :].astype(BF16)
                    vp = vp_ref[prev, :].astype(BF16)
                    prev_ok = jnp.logical_not(first_chunk)
                o, lse = _band_block(q, kc, vc, kp, vp, prev_ok, mask_cur, mask_prev)
                og[g][cur, :] = o
                lg[g][cur, :] = jnp.broadcast_to(lse, (BLOCK, HEAD_DIM))

    rows_per = 64

    def merge(r, carry):
        sl = pl.ds(pl.multiple_of(r * rows_per, rows_per), rows_per)
        ls = [lg[g][sl, :] for g in range(N_GROUPS)]
        mx = functools.reduce(jnp.maximum, ls)
        es = [jnp.exp(l - mx) for l in ls]
        den = functools.reduce(lambda a, b: a + b, es)
        num = functools.reduce(lambda a, b: a + b,
                               [es[g] * og[g][sl, :] for g in range(N_GROUPS)])
        o_ref[sl, :] = (num / den).astype(o_ref.dtype)
        return carry

    lax.fori_loop(0, BAND_CHUNK // rows_per, merge, 0)


def band_attention(p, seq):
    def colblk(g, kind, h):
        return (g * 3 + kind) * N_HEADS + h

    in_specs = []
    operands = []
    for g, (_, dil) in enumerate(DILATED_GROUPS):
        span = BLOCK * dil
        per = BAND_CHUNK // span
        for kind in range(3):
            in_specs.append(pl.BlockSpec(
                (BAND_CHUNK, HEAD_DIM), lambda h, n, g=g, kind=kind: (n, colblk(g, kind, h))))
            operands.append(p)
        for kind in (1, 2):
            in_specs.append(pl.BlockSpec(
                (span, HEAD_DIM),
                lambda h, n, g=g, kind=kind, per=per: (jnp.maximum(n * per - 1, 0),
                                                       colblk(g, kind, h))))
            operands.append(p)
    return pl.pallas_call(
        _band_kernel,
        grid=(N_HEADS, seq // BAND_CHUNK),
        in_specs=in_specs,
        out_specs=pl.BlockSpec((BAND_CHUNK, HEAD_DIM), lambda h, n: (n, h)),
        out_shape=jax.ShapeDtypeStruct((seq, ATTN_WIDTH), BF16),
        scratch_shapes=[pltpu.VMEM((BAND_CHUNK, HEAD_DIM), F32)] * (2 * N_GROUPS),
        compiler_params=_params(2),
        name="band_attention",
    )(*operands)


def _split_heads_kernel(*refs):
    o_ref = refs[-1]
    for part, x_ref in enumerate(refs[:-1]):
        for h in range(o_ref.shape[2]):
            o_ref[:, part, h, :] = x_ref[:, h * HEAD_DIM:(h + 1) * HEAD_DIM]


def split_heads(p, row0, n_rows, col0s, n_heads):
    width = n_heads * HEAD_DIM
    tr = min(n_rows, 512)
    assert row0 % tr == 0 and n_rows % tr == 0 and all(c % width == 0 for c in col0s)
    return pl.pallas_call(
        _split_heads_kernel,
        grid=(n_rows // tr,),
        in_specs=[pl.BlockSpec((tr, width), lambda i, c=c: (row0 // tr + i, c // width))
                  for c in col0s],
        out_specs=pl.BlockSpec((tr, len(col0s), n_heads, HEAD_DIM), lambda i: (i, 0, 0, 0)),
        out_shape=jax.ShapeDtypeStruct((n_rows, len(col0s), n_heads, HEAD_DIM), F32),
        compiler_params=_params(1),
        name="split_heads",
    )(*([p] * len(col0s)))


SHIFT_CHUNK = 128


def _shift_state_kernel(ps_ref, cur_ref, nxt_ref, o_ref, *, n_new):
    chunk = cur_ref.shape[1]
    last = pl.program_id(1) == pl.num_programs(1) - 1
    o_ref[0, 0:chunk - n_new] = cur_ref[0, n_new:chunk]

    @pl.when(last)
    def _():
        o_ref[0, chunk - n_new:chunk] = ps_ref[0, :, 0, 1:3]

    @pl.when(jnp.logical_not(last))
    def _():
        o_ref[0, chunk - n_new:chunk] = nxt_ref[0]


def shift_state(ps, group, buf):
    nb, n_new = ps.shape[:2]
    window = buf.shape[1]
    chunk = min(SHIFT_CHUNK, window)
    n_chunks = window // chunk
    assert window % chunk == 0 and chunk % n_new == 0
    per = chunk // n_new
    tail = buf.shape[2:]
    zeros = (0,) * len(tail)
    return pl.pallas_call(
        functools.partial(_shift_state_kernel, n_new=n_new),
        grid=(nb, n_chunks),
        in_specs=[pl.BlockSpec((1, n_new, 1) + ps.shape[3:], lambda b, c: (b, 0, group, 0, 0, 0)),
                  pl.BlockSpec((1, chunk) + tail, lambda b, c: (b, c) + zeros),
                  pl.BlockSpec((1, n_new) + tail,
                               lambda b, c: (b, jnp.minimum(c + 1, n_chunks - 1) * per) + zeros)],
        out_specs=pl.BlockSpec((1, chunk) + tail, lambda b, c: (b, c) + zeros),
        out_shape=jax.ShapeDtypeStruct(buf.shape, buf.dtype),
        compiler_params=_params(2),
        name="shift_state",
    )(ps, buf, buf)


def _window_kernel(ps_ref, b0_ref, b1_ref, b2_ref, o_ref, *, n_new):
    t = pl.program_id(1)
    key_i = lax.broadcasted_iota(jnp.int32, (BLOCK, N_HEADS, 1), 0)
    outs, lses = [], []
    for g, ((_, dil), buf_ref) in enumerate(zip(DILATED_GROUPS, (b0_ref, b1_ref, b2_ref))):
        h0 = g * 3 * N_HEADS
        q = ps_ref[0, t, h0: h0 + N_HEADS, :]
        k = buf_ref[0, :, 0, 0]
        v = buf_ref[0, :, 0, 1]
        s_past = jnp.sum(k * q[None], axis=-1, keepdims=True) * ATTN_SCALE
        if dil == 1:
            s_past = jnp.where(key_i >= t, s_past, NEG_INF)
        m = jnp.max(s_past, axis=0)
        s_new = []
        for r in range(n_new):
            kn = ps_ref[0, r, h0 + N_HEADS: h0 + 2 * N_HEADS, :]
            sr = jnp.sum(q * kn, axis=-1, keepdims=True) * ATTN_SCALE
            ok = (r <= t) if dil == 1 else (r == t)
            sr = jnp.where(ok, sr, NEG_INF)
            s_new.append(sr)
            m = jnp.maximum(m, sr)
        p_past = jnp.exp(s_past - m[None])
        l = jnp.sum(p_past, axis=0)
        o = jnp.sum(p_past * v, axis=0)
        for r in range(n_new):
            pr = jnp.exp(s_new[r] - m)
            l = l + pr
            o = o + pr * ps_ref[0, r, h0 + 2 * N_HEADS: h0 + 3 * N_HEADS, :]
        outs.append(o / l)
        lses.append(m + jnp.log(l))
    mx = functools.reduce(jnp.maximum, lses)
    es = [jnp.exp(l - mx) for l in lses]
    den = functools.reduce(lambda a, b: a + b, es)
    num = functools.reduce(lambda a, b: a + b, [e * o for e, o in zip(es, outs)])
    merged = num / den
    for h in range(N_HEADS):
        o_ref[0, 0, :, h * HEAD_DIM:(h + 1) * HEAD_DIM] = merged[h:h + 1, :]


def window_attention(ps, bufs):
    nb, n_new = ps.shape[:2]
    in_specs = [pl.BlockSpec((1, n_new) + ps.shape[2:], lambda b, t: (b, 0, 0, 0))]
    operands = [ps]
    for (window, dil), buf in zip(DILATED_GROUPS, bufs):
        assert buf.shape[1] == window == BLOCK * dil
        operands.append(buf.reshape(nb, BLOCK, dil, 2, N_HEADS, HEAD_DIM))
        blk = (1, BLOCK, 1, 2, N_HEADS, HEAD_DIM)
        if dil == 1:
            in_specs.append(pl.BlockSpec(blk, lambda b, t: (b, 0, 0, 0, 0, 0)))
        else:
            in_specs.append(pl.BlockSpec(blk, lambda b, t: (b, 0, t, 0, 0, 0)))
    out = pl.pallas_call(
        functools.partial(_window_kernel, n_new=n_new),
        grid=(nb, n_new),
        in_specs=in_specs,
        out_specs=pl.BlockSpec((1, 1, 1, ATTN_WIDTH), lambda b, t: (b, t, 0, 0)),
        out_shape=jax.ShapeDtypeStruct((nb, n_new, 1, ATTN_WIDTH), F32),
        compiler_params=_params(2),
        name="window_attention",
    )(*operands)
    return out.reshape(nb * n_new, ATTN_WIDTH)


SB_TQ = 512
SB_TK = 256
SB_HEADS = 2
LOG2_E = 1.4426950408889634


def _suffix_matrix(n):
    r = lax.broadcasted_iota(jnp.int32, (2 * n, n), 0) % n
    c = lax.broadcasted_iota(jnp.int32, (2 * n, n), 1)
    return jnp.where(r > c, 1.0, 0.0).astype(BF16)


def _stick_logits(z, causal, tri):
    neg_relu = jnp.minimum(z, 0.0)
    neg_part = neg_relu - z
    soft = jnp.log2(1.0 + jnp.exp2(neg_relu + neg_part))
    log_sig = neg_relu - soft
    u = neg_part - soft
    if causal is not None:
        u = jnp.where(causal, u, 0.0)
    u_hi = u.astype(BF16)
    u_lo = (u - u_hi.astype(F32)).astype(BF16)
    after = jnp.dot(jnp.concatenate([u_hi, u_lo], axis=1), tri, preferred_element_type=F32)
    x = log_sig + after
    if causal is not None:
        x = jnp.where(causal, x, NEG_INF)
    return x, after[:, :1] + u[:, :1]


def _stick_tile(z, causal, carry, tri):
    x, total = _stick_logits(z, causal, tri)
    return jnp.exp2(x + carry), carry + total


def _sb_prompt_kernel(bias_ref, q_ref, k_ref, v_ref, o_ref, k_bf, v_bf):
    hp = pl.program_id(0)
    i = pl.program_id(1)

    @pl.when(i == 0)
    def _():
        k_bf[...] = k_ref[...].astype(BF16)
        v_bf[...] = v_ref[...].astype(BF16)

    cols = [slice(hh * HEAD_DIM, (hh + 1) * HEAD_DIM) for hh in range(SB_HEADS)]
    qs = [q_ref[:, c].astype(BF16) for c in cols]
    biases = [bias_ref[hp * SB_HEADS + hh] * LOG2_E for hh in range(SB_HEADS)]
    tri = _suffix_matrix(SB_TK)
    n_diag = SB_TQ // SB_TK
    n_tiles = (i + 1) * n_diag

    def tile(j, state, masked):
        rows = pl.ds(pl.multiple_of(j * SB_TK, SB_TK), SB_TK)
        causal = None
        if masked:
            q_pos = i * SB_TQ + lax.broadcasted_iota(jnp.int32, (SB_TQ, SB_TK), 0)
            k_pos = j * SB_TK + lax.broadcasted_iota(jnp.int32, (SB_TQ, SB_TK), 1)
            causal = k_pos < q_pos
        out = []
        for hh in range(SB_HEADS):
            carry, acc = state[hh]
            z = _nt_dot(qs[hh], k_bf[rows, cols[hh]]) * (ATTN_SCALE * LOG2_E) + biases[hh]
            a, carry = _stick_tile(z, causal, carry, tri)
            acc = acc + jnp.dot(a.astype(BF16), v_bf[rows, cols[hh]],
                                preferred_element_type=F32)
            out.append((carry, acc))
        return tuple(out)

    state = tuple((jnp.zeros((SB_TQ, 1), F32), jnp.zeros((SB_TQ, HEAD_DIM), F32))
                  for _ in range(SB_HEADS))
    state = lax.fori_loop(0, n_diag, lambda jj, s: tile(n_tiles - 1 - jj, s, True), state)
    state = lax.fori_loop(n_diag, n_tiles, lambda jj, s: tile(n_tiles - 1 - jj, s, False), state)
    for hh in range(SB_HEADS):
        o_ref[:, cols[hh]] = state[hh][1].astype(o_ref.dtype)


def sb_prompt_attention(p, bias, seq):
    width = SB_HEADS * HEAD_DIM
    n_blk = N_HEADS // SB_HEADS
    return pl.pallas_call(
        _sb_prompt_kernel,
        grid=(n_blk, seq // SB_TQ),
        in_specs=[pl.BlockSpec(memory_space=pltpu.SMEM),
                  pl.BlockSpec((SB_TQ, width), lambda h, i: (i, h)),
                  pl.BlockSpec((seq, width), lambda h, i: (0, n_blk + h),
                               pipeline_mode=pl.Buffered(1)),
                  pl.BlockSpec((seq, width), lambda h, i: (0, 2 * n_blk + h),
                               pipeline_mode=pl.Buffered(1))],
        out_specs=pl.BlockSpec((SB_TQ, width), lambda h, i: (i, h)),
        out_shape=jax.ShapeDtypeStruct((seq, ATTN_WIDTH), BF16),
        scratch_shapes=[pltpu.VMEM((seq, width), BF16), pltpu.VMEM((seq, width), BF16)],
        compiler_params=_params(2),
        name="sb_prompt",
    )(bias, p, p, p)


SB_ROWS = 16


SB_PAGES = 4


def _sb_sample_kernel(*refs, n_new):
    ps_ref, bias_ref = refs[1:3]
    kc_refs = refs[3:3 + SB_PAGES]
    vc_refs = refs[3 + SB_PAGES:3 + 2 * SB_PAGES]
    o_ref, q_bf, acc_ref, carry_ref, qf_ref, kn_ref, vn_ref = refs[3 + 2 * SB_PAGES:]
    p = pl.program_id(1)
    tri = _suffix_matrix(PAGE_SIZE)
    bias = bias_ref[...]

    def head_rows(ref, h, n):
        return ref[pl.ds(h, n, stride=N_HEADS), :].astype(BF16)

    def tiles(k_refs, v_refs, causal):
        parts = []
        for k_ref in k_refs:
            s = jnp.concatenate(
                [_nt_dot(q_bf[h], head_rows(k_ref, h, PAGE_SIZE)) for h in range(N_HEADS)],
                axis=0)
            parts.append(_stick_logits(s * (ATTN_SCALE * LOG2_E) + bias, causal, tri))
        carry = carry_ref[...]
        for (x, total), v_ref in zip(parts, v_refs):
            a = jnp.exp2(x + carry).astype(BF16)
            carry = carry + total
            for h in range(N_HEADS):
                acc_ref[h] += jnp.dot(a[h * SB_ROWS:(h + 1) * SB_ROWS],
                                      head_rows(v_ref, h, PAGE_SIZE),
                                      preferred_element_type=F32)
        carry_ref[...] = carry

    @pl.when(p == 0)
    def _():
        qf_ref[...] = jnp.zeros_like(qf_ref)
        kn_ref[...] = jnp.zeros_like(kn_ref)
        vn_ref[...] = jnp.zeros_like(vn_ref)
        for r in range(n_new):
            rows = slice(r * N_HEADS, (r + 1) * N_HEADS)
            qf_ref[rows, :] = ps_ref[0, r, 0:N_HEADS, :]
            kn_ref[rows, :] = ps_ref[0, r, N_HEADS:2 * N_HEADS, :]
            vn_ref[rows, :] = ps_ref[0, r, 2 * N_HEADS:3 * N_HEADS, :]
        for h in range(N_HEADS):
            q_bf[h] = head_rows(qf_ref, h, SB_ROWS)
        acc_ref[...] = jnp.zeros_like(acc_ref)
        carry_ref[...] = jnp.zeros_like(carry_ref)
        shape = (N_HEADS * SB_ROWS, PAGE_SIZE)
        step = lax.broadcasted_iota(jnp.int32, shape, 0) % SB_ROWS
        key = lax.broadcasted_iota(jnp.int32, shape, 1)
        tiles([kn_ref], [vn_ref], key < step)

    tiles([r.at[0, 0] for r in kc_refs], [r.at[0, 0] for r in vc_refs], None)

    @pl.when(p == pl.num_programs(1) - 1)
    def _():
        for r in range(n_new):
            for h in range(N_HEADS):
                o_ref[0, r, :, h * HEAD_DIM:(h + 1) * HEAD_DIM] = acc_ref[h, r:r + 1, :]


def sb_sample_attention(ps, cache_k, cache_v, layer, page_table, bias):
    nb, n_new = ps.shape[:2]
    n_pages = page_table.shape[1]
    n_rows = N_HEADS * SB_ROWS
    assert n_new <= SB_ROWS and n_pages % SB_PAGES == 0

    def page(b, p, pt, slot):
        return (layer, pt[b * n_pages + (n_pages - 1 - (p * SB_PAGES + slot))], 0, 0)

    page_rows = PAGE_SIZE * N_HEADS
    page_specs = [pl.BlockSpec((1, 1, page_rows, HEAD_DIM), functools.partial(page, slot=slot))
                  for slot in range(SB_PAGES)]
    caches = [c.reshape(c.shape[0], c.shape[1], page_rows, HEAD_DIM) for c in (cache_k, cache_v)]
    grid_spec = pltpu.PrefetchScalarGridSpec(
        num_scalar_prefetch=1,
        grid=(nb, n_pages // SB_PAGES),
        in_specs=[pl.BlockSpec((1, n_new) + ps.shape[2:], lambda b, p, pt: (b, 0, 0, 0)),
                  pl.BlockSpec((n_rows, 1), lambda b, p, pt: (0, 0))] + page_specs * 2,
        out_specs=pl.BlockSpec((1, n_new, 1, ATTN_WIDTH), lambda b, p, pt: (b, 0, 0, 0)),
        scratch_shapes=[pltpu.VMEM((N_HEADS, SB_ROWS, HEAD_DIM), BF16),
                        pltpu.VMEM((N_HEADS, SB_ROWS, HEAD_DIM), F32),
                        pltpu.VMEM((n_rows, 1), F32),
                        pltpu.VMEM((SB_ROWS * N_HEADS, HEAD_DIM), F32),
                        pltpu.VMEM((page_rows, HEAD_DIM), F32),
                        pltpu.VMEM((page_rows, HEAD_DIM), F32)],
    )
    bias_rows = jnp.repeat(bias.astype(F32) * LOG2_E, SB_ROWS).reshape(n_rows, 1)
    out = pl.pallas_call(
        functools.partial(_sb_sample_kernel, n_new=n_new),
        grid_spec=grid_spec,
        out_shape=jax.ShapeDtypeStruct((nb, n_new, 1, ATTN_WIDTH), F32),
        compiler_params=_params(2),
        name="sb_sample",
    )(page_table.reshape(-1), ps, bias_rows,
      *([caches[0]] * SB_PAGES), *([caches[1]] * SB_PAGES))
    return out.reshape(nb * n_new, ATTN_WIDTH)


def _rope_tables(pos):
    half = HEAD_DIM // 2
    inv = ROPE_THETA ** (-jnp.arange(half, dtype=F32) / half)
    ang = pos.astype(F32)[:, None] * inv[None, :]
    cos = jnp.cos(ang)
    sin = jnp.sin(ang)
    return jnp.concatenate([cos, cos], axis=-1), jnp.concatenate([-sin, sin], axis=-1)


def _macaron_half(x, g, w_gu, w_d, layer):
    h = rmsnorm(x, g, BF16, tm=640)
    a = gateup(h, w_gu, layer, tm=832, tn=512)
    return mm_resid(a, w_d, layer, x, 0.5, tm=416, tn=512)


def kernel(x_prompt, x_sample, state_win128_kv, state_win512_kv, state_win2048_kv, cache_sb_k, cache_sb_v, page_table, norm_pre, ffn_pre_wgu, ffn_pre_wd, norm_mix, a_w_in, a_w_out, b_w_qkv, b_logit_bias, b_w_out, norm_post, ffn_post_wgu, ffn_post_wd, final_norm):
    _, seq, d_model = x_prompt.shape
    nb, n_new, _ = x_sample.shape
    depth = norm_pre.shape[0]
    past_len = page_table.shape[1] * PAGE_SIZE
    win_bufs = (state_win128_kv, state_win512_kv, state_win2048_kv)

    y = jnp.concatenate([x_prompt.reshape(seq, d_model), x_sample.reshape(nb * n_new, d_model)])
    pos = jnp.concatenate([jnp.arange(seq), past_len + jnp.tile(jnp.arange(n_new), nb)])
    cos2, sin2 = _rope_tables(pos)

    win_p = [[] for _ in DILATED_GROUPS]
    win_s = [[] for _ in DILATED_GROUPS]
    sbk_p, sbv_p, sbk_s, sbv_s = [], [], [], []
    for i in range(depth):
        j = i // N_MIXERS
        y = _macaron_half(y, norm_pre[i], ffn_pre_wgu, ffn_pre_wd, i)
        h = rmsnorm(y, norm_mix[i], BF16, tm=640)
        if i % N_MIXERS == 0:
            p = proj(h, a_w_in, j, cos2, sin2, True, tm=832, tn=1024)
            n_slots = p.shape[1] // HEAD_DIM
            ps = split_heads(p, seq, nb * n_new, (0,), n_slots)
            bufs = tuple(buf[j] for buf in win_bufs)
            for (window, _), buf in zip(DILATED_GROUPS, bufs):
                assert buf.shape[1] == window and seq >= window
            o_s = window_attention(ps.reshape(nb, n_new, n_slots, HEAD_DIM), bufs)
            o = jnp.concatenate([band_attention(p, seq), o_s.astype(BF16)])
            y = mm_resid(o, a_w_out, j, y, 1.0, tm=832, tn=1024)
            ps6 = ps.reshape(nb, n_new, N_GROUPS, 3, N_HEADS, HEAD_DIM)
            for g, (window, _) in enumerate(DILATED_GROUPS):
                cols = tuple((g * 3 + kind) * ATTN_WIDTH for kind in (1, 2))
                win_p[g].append(split_heads(p, seq - window, window, cols, N_HEADS)[None])
                win_s[g].append(shift_state(ps6, g, bufs[g]))
        else:
            p = proj(h, b_w_qkv, j, cos2, sin2, False, tm=832, tn=1024)
            ps = split_heads(p, seq, nb * n_new, (0,), 3 * N_HEADS).reshape(
                nb, n_new, 3 * N_HEADS, HEAD_DIM)
            o_s = sb_sample_attention(ps, cache_sb_k, cache_sb_v, j, page_table, b_logit_bias[j])
            o = jnp.concatenate([sb_prompt_attention(p, b_logit_bias[j].astype(F32), seq),
                                 o_s.astype(BF16)])
            y = mm_resid(o, b_w_out, j, y, 1.0, tm=832, tn=1024)
            sbk_p.append(split_heads(p, 0, seq, (ATTN_WIDTH,), N_HEADS).reshape(
                1, seq, N_HEADS, HEAD_DIM))
            sbv_p.append(split_heads(p, 0, seq, (2 * ATTN_WIDTH,), N_HEADS).reshape(
                1, seq, N_HEADS, HEAD_DIM))
            sbk_s.append(ps[:, :, N_HEADS:2 * N_HEADS])
            sbv_s.append(ps[:, :, 2 * N_HEADS:])
        y = _macaron_half(y, norm_post[i], ffn_post_wgu, ffn_post_wd, i)
    y = rmsnorm(y, final_norm, F32, tm=640)
    return (y[:seq].reshape(1, seq, d_model), y[seq:].reshape(nb, n_new, d_model),
            jnp.stack(win_p[0]), jnp.stack(win_s[0]),
            jnp.stack(win_p[1]), jnp.stack(win_s[1]),
            jnp.stack(win_p[2]), jnp.stack(win_s[2]),
            jnp.stack(sbk_p), jnp.stack(sbv_p), jnp.stack(sbk_s), jnp.stack(sbv_s))
```

```python
import functools

import jax
import jax.numpy as jnp
from jax import lax
from jax.experimental import pallas as pl
from jax.experimental.pallas import tpu as pltpu

F32 = jnp.float32
BF16 = jnp.bfloat16

N_HEADS = 16
HEAD_DIM = 128
ATTN_WIDTH = N_HEADS * HEAD_DIM
ATTN_SCALE = HEAD_DIM ** -0.5
DILATED_GROUPS = ((128, 1), (512, 4), (2048, 16))
N_GROUPS = len(DILATED_GROUPS)
N_MIXERS = 2
BLOCK = 128
PAGE_SIZE = 128
ROPE_THETA = 10000.0
RMS_EPS = 1e-6

V7X_VMEM_LIMIT_BYTES = 56 * 1024 * 1024
NEG_INF = float("-inf")


def _params(n_axes):
    return pltpu.CompilerParams(
        dimension_semantics=("arbitrary",) * n_axes,
        vmem_limit_bytes=V7X_VMEM_LIMIT_BYTES,
    )


def _nt_dot(a, b):
    return lax.dot_general(a, b, (((1,), (1,)), ((), ())), preferred_element_type=F32)


def _rmsnorm_kernel(x_ref, g_ref, o_ref):
    x = x_ref[...]
    ms = jnp.mean(x * x, axis=-1, keepdims=True)
    o_ref[...] = (x * lax.rsqrt(ms + RMS_EPS) * g_ref[...]).astype(o_ref.dtype)


def rmsnorm(x, g, out_dtype, tm):
    m, d = x.shape
    return pl.pallas_call(
        _rmsnorm_kernel,
        grid=(m // tm,),
        in_specs=[pl.BlockSpec((tm, d), lambda i: (i, 0)),
                  pl.BlockSpec((1, d), lambda i: (0, 0))],
        out_specs=pl.BlockSpec((tm, d), lambda i: (i, 0)),
        out_shape=jax.ShapeDtypeStruct((m, d), out_dtype),
        compiler_params=_params(1),
        name="rmsnorm",
    )(x, g.reshape(1, d))


def _gateup_kernel(h_ref, wg_ref, wu_ref, o_ref, wg_bf, wu_bf):
    @pl.when(pl.program_id(1) == 0)
    def _():
        wg_bf[...] = wg_ref[...].astype(BF16)
        wu_bf[...] = wu_ref[...].astype(BF16)

    h = h_ref[...]
    g = jnp.dot(h, wg_bf[...], preferred_element_type=F32)
    u = jnp.dot(h, wu_bf[...], preferred_element_type=F32)
    o_ref[...] = (g * jax.nn.sigmoid(g) * u).astype(o_ref.dtype)


def gateup(h, w_gu, layer, tm, tn):
    m, k = h.shape
    f = w_gu.shape[2] // 2
    nf = f // tn
    return pl.pallas_call(
        _gateup_kernel,
        grid=(nf, m // tm),
        in_specs=[pl.BlockSpec((tm, k), lambda j, i: (i, 0)),
                  pl.BlockSpec((None, k, tn), lambda j, i: (layer, 0, j)),
                  pl.BlockSpec((None, k, tn), lambda j, i: (layer, 0, j + nf))],
        out_specs=pl.BlockSpec((tm, tn), lambda j, i: (i, j)),
        out_shape=jax.ShapeDtypeStruct((m, f), BF16),
        scratch_shapes=[pltpu.VMEM((k, tn), BF16), pltpu.VMEM((k, tn), BF16)],
        compiler_params=_params(2),
        name="ffn_gateup",
    )(h, w_gu, w_gu)


def _mm_resid_kernel(a_ref, w_ref, r_ref, o_ref, w_bf, *, scale):
    @pl.when(pl.program_id(1) == 0)
    def _():
        w_bf[...] = w_ref[...].astype(BF16)

    acc = jnp.dot(a_ref[...], w_bf[...], preferred_element_type=F32)
    o_ref[...] = r_ref[...] + scale * acc


def mm_resid(a, w, layer, r, scale, tm, tn):
    m, k = a.shape
    n = w.shape[2]
    return pl.pallas_call(
        functools.partial(_mm_resid_kernel, scale=scale),
        grid=(n // tn, m // tm),
        in_specs=[pl.BlockSpec((tm, k), lambda j, i: (i, 0)),
                  pl.BlockSpec((None, k, tn), lambda j, i: (layer, 0, j)),
                  pl.BlockSpec((tm, tn), lambda j, i: (i, j))],
        out_specs=pl.BlockSpec((tm, tn), lambda j, i: (i, j)),
        out_shape=jax.ShapeDtypeStruct((m, n), F32),
        scratch_shapes=[pltpu.VMEM((k, tn), BF16)],
        compiler_params=_params(2),
        name="mm_resid",
    )(a, w, r)


def _proj_kernel(h_ref, w_ref, cos_ref, sin_ref, o_ref, w_bf, *, rope, tn):
    @pl.when(pl.program_id(1) == 0)
    def _():
        w_bf[...] = w_ref[...].astype(BF16)

    acc = jnp.dot(h_ref[...], w_bf[...], preferred_element_type=F32)
    if not rope:
        o_ref[...] = acc
        return

    kind = (pl.program_id(0) * tn // ATTN_WIDTH) % 3

    @pl.when(kind < 2)
    def _():
        c = cos_ref[...]
        s = sin_ref[...]
        for hh in range(tn // HEAD_DIM):
            x = acc[:, hh * HEAD_DIM:(hh + 1) * HEAD_DIM]
            o_ref[:, hh * HEAD_DIM:(hh + 1) * HEAD_DIM] = (
                x * c + pltpu.roll(x, HEAD_DIM // 2, 1) * s)

    @pl.when(kind == 2)
    def _():
        o_ref[...] = acc


def proj(h, w, layer, cos2, sin2, rope, tm, tn):
    m, k = h.shape
    n = w.shape[2]
    return pl.pallas_call(
        functools.partial(_proj_kernel, rope=rope, tn=tn),
        grid=(n // tn, m // tm),
        in_specs=[pl.BlockSpec((tm, k), lambda j, i: (i, 0)),
                  pl.BlockSpec((None, k, tn), lambda j, i: (layer, 0, j)),
                  pl.BlockSpec((tm, HEAD_DIM), lambda j, i: (i, 0)),
                  pl.BlockSpec((tm, HEAD_DIM), lambda j, i: (i, 0))],
        out_specs=pl.BlockSpec((tm, tn), lambda j, i: (i, j)),
        out_shape=jax.ShapeDtypeStruct((m, n), F32),
        scratch_shapes=[pltpu.VMEM((k, tn), BF16)],
        compiler_params=_params(2),
        name="proj_rope" if rope else "proj",
    )(h, w, cos2, sin2)


BAND_CHUNK = BLOCK * max(d for _, d in DILATED_GROUPS)


def _band_block(q, kc, vc, kp, vp, prev_ok, mask_cur, mask_prev):
    mp = mask_prev if prev_ok is None else jnp.logical_and(mask_prev, prev_ok)
    mask = jnp.concatenate([mp, mask_cur], axis=1)
    s = _nt_dot(q, jnp.concatenate([kp, kc], axis=0)) * ATTN_SCALE
    s = jnp.where(mask, s, NEG_INF)
    m = jnp.max(s, axis=-1, keepdims=True)
    p = jnp.exp(s - m)
    l = jnp.sum(p, axis=-1, keepdims=True)
    acc = jnp.dot(p.astype(BF16), jnp.concatenate([vp, vc], axis=0),
                  preferred_element_type=F32)
    return acc / l, m + jnp.log(l)


def _band_kernel(*refs):
    ins = refs[:5 * N_GROUPS]
    o_ref = refs[5 * N_GROUPS]
    og = refs[5 * N_GROUPS + 1: 5 * N_GROUPS + 1 + N_GROUPS]
    lg = refs[5 * N_GROUPS + 1 + N_GROUPS:]
    first_chunk = pl.program_id(1) == 0
    row = lax.broadcasted_iota(jnp.int32, (BLOCK, BLOCK), 0)
    col = lax.broadcasted_iota(jnp.int32, (BLOCK, BLOCK), 1)
    mask_cur = col <= row
    mask_prev = col >= row

    for g, (_, dil) in enumerate(DILATED_GROUPS):
        q_ref, kc_ref, vc_ref, kp_ref, vp_ref = ins[5 * g: 5 * g + 5]
        span = BLOCK * dil
        for s in range(BAND_CHUNK // span):
            for c in range(dil):
                def rows(start):
                    if dil == 1:
                        return pl.ds(start, BLOCK)
                    return pl.ds(start, BLOCK, stride=dil)
                cur = rows(s * span + c)
                q = q_ref[cur, :].astype(BF16)
                kc = kc_ref[cur, :].astype(BF16)
                vc = vc_ref[cur, :].astype(BF16)
                if s > 0:
                    prev = rows((s - 1) * span + c)
                    kp = kc_ref[prev, :].astype(BF16)
                    vp = vc_ref[prev, :].astype(BF16)
                    prev_ok = None
                else:
                    prev = rows(c)
                    kp = kp_ref[prev, :].astype(BF16)
                    vp = vp_ref[prev, :].astype(BF16)
                    prev_ok = jnp.logical_not(first_chunk)
                o, lse = _band_block(q, kc, vc, kp, vp, prev_ok, mask_cur, mask_prev)
                og[g][cur, :] = o
                lg[g][cur, :] = jnp.broadcast_to(lse, (BLOCK, HEAD_DIM))

    rows_per = 64

    def merge(r, carry):
        sl = pl.ds(pl.multiple_of(r * rows_per, rows_per), rows_per)
        ls = [lg[g][sl, :] for g in range(N_GROUPS)]
        mx = functools.reduce(jnp.maximum, ls)
        es = [jnp.exp(l - mx) for l in ls]
        den = functools.reduce(lambda a, b: a + b, es)
        num = functools.reduce(lambda a, b: a + b,
                               [es[g] * og[g][sl, :] for g in range(N_GROUPS)])
        o_ref[sl, :] = (num / den).astype(o_ref.dtype)
        return carry

    lax.fori_loop(0, BAND_CHUNK // rows_per, merge, 0)


def band_attention(p, seq):
    def colblk(g, kind, h):
        return (g * 3 + kind) * N_HEADS + h

    in_specs = []
    operands = []
    for g, (_, dil) in enumerate(DILATED_GROUPS):
        span = BLOCK * dil
        per = BAND_CHUNK // span
        for kind in range(3):
            in_specs.append(pl.BlockSpec(
                (BAND_CHUNK, HEAD_DIM), lambda h, n, g=g, kind=kind: (n, colblk(g, kind, h))))
            operands.append(p)
        for kind in (1, 2):
            in_specs.append(pl.BlockSpec(
                (span, HEAD_DIM),
                lambda h, n, g=g, kind=kind, per=per: (jnp.maximum(n * per - 1, 0),
                                                       colblk(g, kind, h))))
            operands.append(p)
    return pl.pallas_call(
        _band_kernel,
        grid=(N_HEADS, seq // BAND_CHUNK),
        in_specs=in_specs,
        out_specs=pl.BlockSpec((BAND_CHUNK, HEAD_DIM), lambda h, n: (n, h)),
        out_shape=jax.ShapeDtypeStruct((seq, ATTN_WIDTH), BF16),
        scratch_shapes=[pltpu.VMEM((BAND_CHUNK, HEAD_DIM), F32)] * (2 * N_GROUPS),
        compiler_params=_params(2),
        name="band_attention",
    )(*operands)


def _split_heads_kernel(*refs):
    o_ref = refs[-1]
    for part, x_ref in enumerate(refs[:-1]):
        for h in range(o_ref.shape[2]):
            o_ref[:, part, h, :] = x_ref[:, h * HEAD_DIM:(h + 1) * HEAD_DIM]


def split_heads(p, row0, n_rows, col0s, n_heads):
    width = n_heads * HEAD_DIM
    tr = min(n_rows, 512)
    assert row0 % tr == 0 and n_rows % tr == 0 and all(c % width == 0 for c in col0s)
    return pl.pallas_call(
        _split_heads_kernel,
        grid=(n_rows // tr,),
        in_specs=[pl.BlockSpec((tr, width), lambda i, c=c: (row0 // tr + i, c // width))
                  for c in col0s],
        out_specs=pl.BlockSpec((tr, len(col0s), n_heads, HEAD_DIM), lambda i: (i, 0, 0, 0)),
        out_shape=jax.ShapeDtypeStruct((n_rows, len(col0s), n_heads, HEAD_DIM), F32),
        compiler_params=_params(1),
        name="split_heads",
    )(*([p] * len(col0s)))


SHIFT_CHUNK = 128


def _shift_state_kernel(ps_ref, cur_ref, nxt_ref, o_ref, *, n_new):
    chunk = cur_ref.shape[1]
    last = pl.program_id(1) == pl.num_programs(1) - 1
    o_ref[0, 0:chunk - n_new] = cur_ref[0, n_new:chunk]

    @pl.when(last)
    def _():
        o_ref[0, chunk - n_new:chunk] = ps_ref[0, :, 0, 1:3]

    @pl.when(jnp.logical_not(last))
    def _():
        o_ref[0, chunk - n_new:chunk] = nxt_ref[0]


def shift_state(ps, group, buf):
    nb, n_new = ps.shape[:2]
    window = buf.shape[1]
    chunk = min(SHIFT_CHUNK, window)
    n_chunks = window // chunk
    assert window % chunk == 0 and chunk % n_new == 0
    per = chunk // n_new
    tail = buf.shape[2:]
    zeros = (0,) * len(tail)
    return pl.pallas_call(
        functools.partial(_shift_state_kernel, n_new=n_new),
        grid=(nb, n_chunks),
        in_specs=[pl.BlockSpec((1, n_new, 1) + ps.shape[3:], lambda b, c: (b, 0, group, 0, 0, 0)),
                  pl.BlockSpec((1, chunk) + tail, lambda b, c: (b, c) + zeros),
                  pl.BlockSpec((1, n_new) + tail,
                               lambda b, c: (b, jnp.minimum(c + 1, n_chunks - 1) * per) + zeros)],
        out_specs=pl.BlockSpec((1, chunk) + tail, lambda b, c: (b, c) + zeros),
        out_shape=jax.ShapeDtypeStruct(buf.shape, buf.dtype),
        compiler_params=_params(2),
        name="shift_state",
    )(ps, buf, buf)


def _window_kernel(ps_ref, b0_ref, b1_ref, b2_ref, o_ref, *, n_new):
    t = pl.program_id(1)
    key_i = lax.broadcasted_iota(jnp.int32, (BLOCK, N_HEADS, 1), 0)
    outs, lses = [], []
    for g, ((_, dil), buf_ref) in enumerate(zip(DILATED_GROUPS, (b0_ref, b1_ref, b2_ref))):
        h0 = g * 3 * N_HEADS
        q = ps_ref[0, t, h0: h0 + N_HEADS, :]
        k = buf_ref[0, :, 0, 0]
        v = buf_ref[0, :, 0, 1]
        s_past = jnp.sum(k * q[None], axis=-1, keepdims=True) * ATTN_SCALE
        if dil == 1:
            s_past = jnp.where(key_i >= t, s_past, NEG_INF)
        m = jnp.max(s_past, axis=0)
        s_new = []
        for r in range(n_new):
            kn = ps_ref[0, r, h0 + N_HEADS: h0 + 2 * N_HEADS, :]
            sr = jnp.sum(q * kn, axis=-1, keepdims=True) * ATTN_SCALE
            ok = (r <= t) if dil == 1 else (r == t)
            sr = jnp.where(ok, sr, NEG_INF)
            s_new.append(sr)
            m = jnp.maximum(m, sr)
        p_past = jnp.exp(s_past - m[None])
        l = jnp.sum(p_past, axis=0)
        o = jnp.sum(p_past * v, axis=0)
        for r in range(n_new):
            pr = jnp.exp(s_new[r] - m)
            l = l + pr
            o = o + pr * ps_ref[0, r, h0 + 2 * N_HEADS: h0 + 3 * N_HEADS, :]
        outs.append(o / l)
        lses.append(m + jnp.log(l))
    mx = functools.reduce(jnp.maximum, lses)
    es = [jnp.exp(l - mx) for l in lses]
    den = functools.reduce(lambda a, b: a + b, es)
    num = functools.reduce(lambda a, b: a + b, [e * o for e, o in zip(es, outs)])
    merged = num / den
    for h in range(N_HEADS):
        o_ref[0, 0, :, h * HEAD_DIM:(h + 1) * HEAD_DIM] = merged[h:h + 1, :]


def window_attention(ps, bufs):
    nb, n_new = ps.shape[:2]
    in_specs = [pl.BlockSpec((1, n_new) + ps.shape[2:], lambda b, t: (b, 0, 0, 0))]
    operands = [ps]
    for (window, dil), buf in zip(DILATED_GROUPS, bufs):
        assert buf.shape[1] == window == BLOCK * dil
        operands.append(buf.reshape(nb, BLOCK, dil, 2, N_HEADS, HEAD_DIM))
        blk = (1, BLOCK, 1, 2, N_HEADS, HEAD_DIM)
        if dil == 1:
            in_specs.append(pl.BlockSpec(blk, lambda b, t: (b, 0, 0, 0, 0, 0)))
        else:
            in_specs.append(pl.BlockSpec(blk, lambda b, t: (b, 0, t, 0, 0, 0)))
    out = pl.pallas_call(
        functools.partial(_window_kernel, n_new=n_new),
        grid=(nb, n_new),
        in_specs=in_specs,
        out_specs=pl.BlockSpec((1, 1, 1, ATTN_WIDTH), lambda b, t: (b, t, 0, 0)),
        out_shape=jax.ShapeDtypeStruct((nb, n_new, 1, ATTN_WIDTH), F32),
        compiler_params=_params(2),
        name="window_attention",
    )(*operands)
    return out.reshape(nb * n_new, ATTN_WIDTH)


SB_TQ = 1024
SB_TK = 256
SB_HEADS = 1
LOG2_E = 1.4426950408889634


def _suffix_matrix(n):
    r = lax.broadcasted_iota(jnp.int32, (2 * n, n), 0) % n
    c = lax.broadcasted_iota(jnp.int32, (2 * n, n), 1)
    return jnp.where(r > c, 1.0, 0.0).astype(BF16)


def _stick_logits(z, causal, tri):
    neg_relu = jnp.minimum(z, 0.0)
    neg_part = neg_relu - z
    soft = jnp.log2(1.0 + jnp.exp2(neg_relu + neg_part))
    log_sig = neg_relu - soft
    u = neg_part - soft
    if causal is not None:
        u = jnp.where(causal, u, 0.0)
    u_hi = u.astype(BF16)
    u_lo = (u - u_hi.astype(F32)).astype(BF16)
    after = jnp.dot(jnp.concatenate([u_hi, u_lo], axis=1), tri, preferred_element_type=F32)
    x = log_sig + after
    if causal is not None:
        x = jnp.where(causal, x, NEG_INF)
    return x, after[:, :1] + u[:, :1]


def _stick_tile(z, causal, carry, tri):
    x, total = _stick_logits(z, causal, tri)
    return jnp.exp2(x + carry), carry + total


def _sb_prompt_kernel(bias_ref, q_ref, k_ref, v_ref, o_ref, k_bf, v_bf):
    hp = pl.program_id(0)
    i = pl.program_id(1)

    @pl.when(i == 0)
    def _():
        k_bf[...] = k_ref[...].astype(BF16)
        v_bf[...] = v_ref[...].astype(BF16)

    cols = [slice(hh * HEAD_DIM, (hh + 1) * HEAD_DIM) for hh in range(SB_HEADS)]
    qs = [q_ref[:, c].astype(BF16) for c in cols]
    biases = [bias_ref[hp * SB_HEADS + hh] * LOG2_E for hh in range(SB_HEADS)]
    tri = _suffix_matrix(SB_TK)
    n_diag = SB_TQ // SB_TK
    n_tiles = (i + 1) * n_diag

    def tile(j, state, masked):
        rows = pl.ds(pl.multiple_of(j * SB_TK, SB_TK), SB_TK)
        causal = None
        if masked:
            q_pos = i * SB_TQ + lax.broadcasted_iota(jnp.int32, (SB_TQ, SB_TK), 0)
            k_pos = j * SB_TK + lax.broadcasted_iota(jnp.int32, (SB_TQ, SB_TK), 1)
            causal = k_pos < q_pos
        out = []
        for hh in range(SB_HEADS):
            carry, acc = state[hh]
            z = _nt_dot(qs[hh], k_bf[rows, cols[hh]]) * (ATTN_SCALE * LOG2_E) + biases[hh]
            a, carry = _stick_tile(z, causal, carry, tri)
            acc = acc + jnp.dot(a.astype(BF16), v_bf[rows, cols[hh]],
                                preferred_element_type=F32)
            out.append((carry, acc))
        return tuple(out)

    state = tuple((jnp.zeros((SB_TQ, 1), F32), jnp.zeros((SB_TQ, HEAD_DIM), F32))
                  for _ in range(SB_HEADS))
    state = lax.fori_loop(0, n_diag, lambda jj, s: tile(n_tiles - 1 - jj, s, True), state)
    state = lax.fori_loop(n_diag, n_tiles, lambda jj, s: tile(n_tiles - 1 - jj, s, False), state)
    for hh in range(SB_HEADS):
        o_ref[:, cols[hh]] = state[hh][1].astype(o_ref.dtype)


def sb_prompt_attention(p, bias, seq):
    width = SB_HEADS * HEAD_DIM
    n_blk = N_HEADS // SB_HEADS
    return pl.pallas_call(
        _sb_prompt_kernel,
        grid=(n_blk, seq // SB_TQ),
        in_specs=[pl.BlockSpec(memory_space=pltpu.SMEM),
                  pl.BlockSpec((SB_TQ, width), lambda h, i: (i, h)),
                  pl.BlockSpec((seq, width), lambda h, i: (0, n_blk + h),
                               pipeline_mode=pl.Buffered(1)),
                  pl.BlockSpec((seq, width), lambda h, i: (0, 2 * n_blk + h),
                               pipeline_mode=pl.Buffered(1))],
        out_specs=pl.BlockSpec((SB_TQ, width), lambda h, i: (i, h)),
        out_shape=jax.ShapeDtypeStruct((seq, ATTN_WIDTH), BF16),
        scratch_shapes=[pltpu.VMEM((seq, width), BF16), pltpu.VMEM((seq, width), BF16)],
        compiler_params=_params(2),
        name="sb_prompt",
    )(bias, p, p, p)


SB_ROWS = 16


SB_PAGES = 4


def _sb_sample_kernel(*refs, n_new):
    ps_ref, bias_ref = refs[1:3]
    kc_refs = refs[3:3 + SB_PAGES]
    vc_refs = refs[3 + SB_PAGES:3 + 2 * SB_PAGES]
    o_ref, q_bf, acc_ref, carry_ref, qf_ref, kn_ref, vn_ref = refs[3 + 2 * SB_PAGES:]
    p = pl.program_id(1)
    tri = _suffix_matrix(PAGE_SIZE)
    bias = bias_ref[...]

    def head_rows(ref, h, n):
        return ref[pl.ds(h, n, stride=N_HEADS), :].astype(BF16)

    def tiles(k_refs, v_refs, causal):
        parts = []
        for k_ref in k_refs:
            s = jnp.concatenate(
                [_nt_dot(q_bf[h], head_rows(k_ref, h, PAGE_SIZE)) for h in range(N_HEADS)],
                axis=0)
            parts.append(_stick_logits(s * (ATTN_SCALE * LOG2_E) + bias, causal, tri))
        carry = carry_ref[...]
        for (x, total), v_ref in zip(parts, v_refs):
            a = jnp.exp2(x + carry).astype(BF16)
            carry = carry + total
            for h in range(N_HEADS):
                acc_ref[h] += jnp.dot(a[h * SB_ROWS:(h + 1) * SB_ROWS],
                                      head_rows(v_ref, h, PAGE_SIZE),
                                      preferred_element_type=F32)
        carry_ref[...] = carry

    @pl.when(p == 0)
    def _():
        qf_ref[...] = jnp.zeros_like(qf_ref)
        kn_ref[...] = jnp.zeros_like(kn_ref)
        vn_ref[...] = jnp.zeros_like(vn_ref)
        for r in range(n_new):
            rows = slice(r * N_HEADS, (r + 1) * N_HEADS)
            qf_ref[rows, :] = ps_ref[0, r, 0:N_HEADS, :]
            kn_ref[rows, :] = ps_ref[0, r, N_HEADS:2 * N_HEADS, :]
            vn_ref[rows, :] = ps_ref[0, r, 2 * N_HEADS:3 * N_HEADS, :]
        for h in range(N_HEADS):
            q_bf[h] = head_rows(qf_ref, h, SB_ROWS)
        acc_ref[...] = jnp.zeros_like(acc_ref)
        carry_ref[...] = jnp.zeros_like(carry_ref)
        shape = (N_HEADS * SB_ROWS, PAGE_SIZE)
        step = lax.broadcasted_iota(jnp.int32, shape, 0) % SB_ROWS
        key = lax.broadcasted_iota(jnp.int32, shape, 1)
        tiles([kn_ref], [vn_ref], key < step)

    tiles([r.at[0, 0] for r in kc_refs], [r.at[0, 0] for r in vc_refs], None)

    @pl.when(p == pl.num_programs(1) - 1)
    def _():
        for r in range(n_new):
            for h in range(N_HEADS):
                o_ref[0, r, :, h * HEAD_DIM:(h + 1) * HEAD_DIM] = acc_ref[h, r:r + 1, :]


def sb_sample_attention(ps, cache_k, cache_v, layer, page_table, bias):
    nb, n_new = ps.shape[:2]
    n_pages = page_table.shape[1]
    n_rows = N_HEADS * SB_ROWS
    assert n_new <= SB_ROWS and n_pages % SB_PAGES == 0

    def page(b, p, pt, slot):
        return (layer, pt[b * n_pages + (n_pages - 1 - (p * SB_PAGES + slot))], 0, 0)

    page_rows = PAGE_SIZE * N_HEADS
    page_specs = [pl.BlockSpec((1, 1, page_rows, HEAD_DIM), functools.partial(page, slot=slot))
                  for slot in range(SB_PAGES)]
    caches = [c.reshape(c.shape[0], c.shape[1], page_rows, HEAD_DIM) for c in (cache_k, cache_v)]
    grid_spec = pltpu.PrefetchScalarGridSpec(
        num_scalar_prefetch=1,
        grid=(nb, n_pages // SB_PAGES),
        in_specs=[pl.BlockSpec((1, n_new) + ps.shape[2:], lambda b, p, pt: (b, 0, 0, 0)),
                  pl.BlockSpec((n_rows, 1), lambda b, p, pt: (0, 0))] + page_specs * 2,
        out_specs=pl.BlockSpec((1, n_new, 1, ATTN_WIDTH), lambda b, p, pt: (b, 0, 0, 0)),
        scratch_shapes=[pltpu.VMEM((N_HEADS, SB_ROWS, HEAD_DIM), BF16),
                        pltpu.VMEM((N_HEADS, SB_ROWS, HEAD_DIM), F32),
                        pltpu.VMEM((n_rows, 1), F32),
                        pltpu.VMEM((SB_ROWS * N_HEADS, HEAD_DIM), F32),
                        pltpu.VMEM((page_rows, HEAD_DIM), F32),
                        pltpu.VMEM((page_rows, HEAD_DIM), F32)],
    )
    bias_rows = jnp.repeat(bias.astype(F32) * LOG2_E, SB_ROWS).reshape(n_rows, 1)
    out = pl.pallas_call(
        functools.partial(_sb_sample_kernel, n_new=n_new),
        grid_spec=grid_spec,
        out_shape=jax.ShapeDtypeStruct((nb, n_new, 1, ATTN_WIDTH), F32),
        compiler_params=_params(2),
        name="sb_sample",
    )(page_table.reshape(-1), ps, bias_rows,
      *([caches[0]] * SB_PAGES), *([caches[1]] * SB_PAGES))
    return out.reshape(nb * n_new, ATTN_WIDTH)


def _rope_tables(pos):
    half = HEAD_DIM // 2
    inv = ROPE_THETA ** (-jnp.arange(half, dtype=F32) / half)
    ang = pos.astype(F32)[:, None] * inv[None, :]
    cos = jnp.cos(ang)
    sin = jnp.sin(ang)
    return jnp.concatenate([cos, cos], axis=-1), jnp.concatenate([-sin, sin], axis=-1)


def _macaron_half(x, g, w_gu, w_d, layer):
    h = rmsnorm(x, g, BF16, tm=640)
    a = gateup(h, w_gu, layer, tm=832, tn=512)
    return mm_resid(a, w_d, layer, x, 0.5, tm=416, tn=512)


def kernel(x_prompt, x_sample, state_win128_kv, state_win512_kv, state_win2048_kv, cache_sb_k, cache_sb_v, page_table, norm_pre, ffn_pre_wgu, ffn_pre_wd, norm_mix, a_w_in, a_w_out, b_w_qkv, b_logit_bias, b_w_out, norm_post, ffn_post_wgu, ffn_post_wd, final_norm):
    _, seq, d_model = x_prompt.shape
    nb, n_new, _ = x_sample.shape
    depth = norm_pre.shape[0]
    past_len = page_table.shape[1] * PAGE_SIZE
    win_bufs = (state_win128_kv, state_win512_kv, state_win2048_kv)

    y = jnp.concatenate([x_prompt.reshape(seq, d_model), x_sample.reshape(nb * n_new, d_model)])
    pos = jnp.concatenate([jnp.arange(seq), past_len + jnp.tile(jnp.arange(n_new), nb)])
    cos2, sin2 = _rope_tables(pos)

    win_p = [[] for _ in DILATED_GROUPS]
    win_s = [[] for _ in DILATED_GROUPS]
    sbk_p, sbv_p, sbk_s, sbv_s = [], [], [], []
    for i in range(depth):
        j = i // N_MIXERS
        y = _macaron_half(y, norm_pre[i], ffn_pre_wgu, ffn_pre_wd, i)
        h = rmsnorm(y, norm_mix[i], BF16, tm=640)
        if i % N_MIXERS == 0:
            p = proj(h, a_w_in, j, cos2, sin2, True, tm=832, tn=1024)
            n_slots = p.shape[1] // HEAD_DIM
            ps = split_heads(p, seq, nb * n_new, (0,), n_slots)
            bufs = tuple(buf[j] for buf in win_bufs)
            for (window, _), buf in zip(DILATED_GROUPS, bufs):
                assert buf.shape[1] == window and seq >= window
            o_s = window_attention(ps.reshape(nb, n_new, n_slots, HEAD_DIM), bufs)
            o = jnp.concatenate([band_attention(p, seq), o_s.astype(BF16)])
            y = mm_resid(o, a_w_out, j, y, 1.0, tm=832, tn=1024)
            ps6 = ps.reshape(nb, n_new, N_GROUPS, 3, N_HEADS, HEAD_DIM)
            for g, (window, _) in enumerate(DILATED_GROUPS):
                cols = tuple((g * 3 + kind) * ATTN_WIDTH for kind in (1, 2))
                win_p[g].append(split_heads(p, seq - window, window, cols, N_HEADS)[None])
                win_s[g].append(shift_state(ps6, g, bufs[g]))
        else:
            p = proj(h, b_w_qkv, j, cos2, sin2, False, tm=832, tn=1024)
            ps = split_heads(p, seq, nb * n_new, (0,), 3 * N_HEADS).reshape(
                nb, n_new, 3 * N_HEADS, HEAD_DIM)
            o_s = sb_sample_attention(ps, cache_sb_k, cache_sb_v, j, page_table, b_logit_bias[j])
            o = jnp.concatenate([sb_prompt_attention(p, b_logit_bias[j].astype(F32), seq),
                                 o_s.astype(BF16)])
            y = mm_resid(o, b_w_out, j, y, 1.0, tm=832, tn=1024)
            sbk_p.append(split_heads(p, 0, seq, (ATTN_WIDTH,), N_HEADS).reshape(
                1, seq, N_HEADS, HEAD_DIM))
            sbv_p.append(split_heads(p, 0, seq, (2 * ATTN_WIDTH,), N_HEADS).reshape(
                1, seq, N_HEADS, HEAD_DIM))
            sbk_s.append(ps[:, :, N_HEADS:2 * N_HEADS])
            sbv_s.append(ps[:, :, 2 * N_HEADS:])
        y = _macaron_half(y, norm_post[i], ffn_post_wgu, ffn_post_wd, i)
    y = rmsnorm(y, final_norm, F32, tm=640)
    return (y[:seq].reshape(1, seq, d_model), y[seq:].reshape(nb, n_new, d_model),
            jnp.stack(win_p[0]), jnp.stack(win_s[0]),
            jnp.stack(win_p[1]), jnp.stack(win_s[1]),
            jnp.stack(win_p[2]), jnp.stack(win_s[2]),
            jnp.stack(sbk_p), jnp.stack(sbv_p), jnp.stack(sbk_s), jnp.stack(sbv_s))
```
